```python
import math
import jax, jax.numpy as jnp
from jax import lax
import numpy as np

D_MODEL = 1024
BATCH = 8
SEQ = 8192
DEPTH = 2

N_MIXERS = 2
N_SSM_LAYERS = (DEPTH + 1) // 2
N_ATTN_LAYERS = DEPTH // 2
RMS_EPS = 1e-6

SSM_GROUP = 16
SSM_GROUPS = D_MODEL // SSM_GROUP
SSM_STATE = 64
STEP_MIN = 1e-3
STEP_MAX = 1e-1

ATTN_HEAD_DIM = 64
ATTN_HEADS = D_MODEL // ATTN_HEAD_DIM
DILATED_GROUPS = ((128, 1), (512, 4), (2048, 16))
N_DIL_GROUPS = len(DILATED_GROUPS)
QKV_COLS = N_DIL_GROUPS * 3 * ATTN_HEADS * ATTN_HEAD_DIM
ATTN_BLOCK = 128
ROPE_THETA = 10000.0

PEER_HEADS = 8
PEER_N_KEYS = 128
PEER_N_EXPERTS = PEER_N_KEYS * PEER_N_KEYS
PEER_KEY_DIM = 256
PEER_HALF = PEER_KEY_DIM // 2
PEER_TOPK = 16
PEER_CHUNK = 128

kernel_name = "hybrid_s5_dilated_attn_peer"


def rms_norm(x, g):
    xf = x.astype(jnp.float32)
    y = xf * lax.rsqrt(jnp.mean(xf * xf, axis=-1, keepdims=True) + RMS_EPS)
    return (y * g.astype(jnp.float32)).astype(x.dtype)


def _complex_affine_combine(left, right):
    a1r, a1i, b1r, b1i = left
    a2r, a2i, b2r, b2i = right
    ar = a2r * a1r - a2i * a1i
    ai = a2r * a1i + a2i * a1r
    br = a2r * b1r - a2i * b1i + b2r
    bi = a2r * b1i + a2i * b1r + b2i
    return (ar, ai, br, bi)


def s5_mixer(h, lam_re, lam_im, log_step, b_re, b_im, c_re, c_im, d_skip, w_glu):
    f32 = jnp.float32
    bsz, seq, _ = h.shape
    u = h.astype(f32).reshape(bsz, seq, SSM_GROUPS, SSM_GROUP)
    lam_re = lam_re.astype(f32)
    lam_im = lam_im.astype(f32)
    step = jnp.exp(log_step.astype(f32))[:, None]
    mag = jnp.exp(lam_re * step)
    lb_re = mag * jnp.cos(lam_im * step)
    lb_im = mag * jnp.sin(lam_im * step)
    den = lam_re * lam_re + lam_im * lam_im
    num_re = lb_re - 1.0
    coef_re = (num_re * lam_re + lb_im * lam_im) / den
    coef_im = (lb_im * lam_re - num_re * lam_im) / den
    b_re = b_re.astype(f32)
    b_im = b_im.astype(f32)
    bb_re = coef_re[..., None] * b_re - coef_im[..., None] * b_im
    bb_im = coef_re[..., None] * b_im + coef_im[..., None] * b_re
    bu_re = jnp.einsum('blgh,gph->lbgp', u, bb_re)
    bu_im = jnp.einsum('blgh,gph->lbgp', u, bb_im)
    a_re = jnp.broadcast_to(lb_re, (seq, 1, SSM_GROUPS, SSM_STATE))
    a_im = jnp.broadcast_to(lb_im, (seq, 1, SSM_GROUPS, SSM_STATE))
    _, _, s_re, s_im = lax.associative_scan(_complex_affine_combine,
                                            (a_re, a_im, bu_re, bu_im), axis=0)
    y = (jnp.einsum('lbgp,ghp->blgh', s_re, c_re.astype(f32))
         - jnp.einsum('lbgp,ghp->blgh', s_im, c_im.astype(f32)))
    y = y.reshape(bsz, seq, D_MODEL) + d_skip.astype(f32) * h.astype(f32)
    y = jax.nn.gelu(y).astype(h.dtype)
    z = y @ w_glu
    val, gate = jnp.split(z, 2, axis=-1)
    return val * jax.nn.sigmoid(gate)


def rope(x, positions):
    dh = x.shape[-1]
    half = dh // 2
    inv = ROPE_THETA ** (-jnp.arange(half, dtype=jnp.float32) / half)
    ang = positions.astype(jnp.float32)[:, None] * inv[None, :]
    cos = jnp.cos(ang)[None, :, None, :]
    sin = jnp.sin(ang)[None, :, None, :]
    xf = x.astype(jnp.float32)
    x1, x2 = xf[..., :half], xf[..., half:]
    return jnp.concatenate([x1 * cos - x2 * sin, x2 * cos + x1 * sin], axis=-1).astype(x.dtype)


def dilated_window_attention(q, k, v, dilation, window):
    bsz, seq, nh, dh = q.shape
    m_len = seq // dilation
    c = min(ATTN_BLOCK, m_len)
    nblk = m_len // c
    steps = window // dilation

    def to_residues(t):
        return t.reshape(bsz, m_len, dilation, nh, dh).transpose(0, 2, 1, 3, 4)

    def band(t):
        tp = jnp.pad(t, ((0, 0), (0, 0), (c, 0), (0, 0), (0, 0)))
        tb = tp.reshape(bsz, dilation, nblk + 1, c, nh, dh)
        return jnp.concatenate([tb[:, :, :-1], tb[:, :, 1:]], axis=3)

    qb = to_residues(q).reshape(bsz, dilation, nblk, c, nh, dh)
    kb = band(to_residues(k))
    vb = band(to_residues(v))
    s = jnp.einsum('bdnqhe,bdnkhe->bdnhqk', qb, kb).astype(jnp.float32) * (dh ** -0.5)
    qi = jnp.arange(c)[:, None]
    ki = jnp.arange(2 * c)[None, :]
    dist = qi + c - ki
    blk = jnp.arange(nblk)[:, None, None]
    valid = (dist >= 0) & (dist <= steps) & (blk * c + ki - c >= 0)
    s = jnp.where(valid[:, None], s, -1e30)
    smax = jnp.max(s, axis=-1, keepdims=True)
    e = jnp.exp(s - smax)
    den = jnp.sum(e, axis=-1, keepdims=True)
    p = (e / den).astype(v.dtype)
    lse = smax[..., 0] + jnp.log(den[..., 0])
    o = jnp.einsum('bdnhqk,bdnkhe->bdnqhe', p, vb)
    o = o.transpose(0, 2, 3, 1, 4, 5).reshape(bsz, seq, nh, dh)
    lse = lse.transpose(0, 2, 4, 1, 3).reshape(bsz, seq, nh)
    return o, lse


def dilated_attention_mixer(h, w_qkv, w_o):
    bsz, seq, _ = h.shape
    qkv = (h @ w_qkv).reshape(bsz, seq, N_DIL_GROUPS, 3, ATTN_HEADS, ATTN_HEAD_DIM)
    pos = jnp.arange(seq)
    outs, lses = [], []
    for g, (window, dilation) in enumerate(DILATED_GROUPS):
        q = rope(qkv[:, :, g, 0], pos)
        k = rope(qkv[:, :, g, 1], pos)
        v = qkv[:, :, g, 2]
        o, lse = dilated_window_attention(q, k, v, dilation, window)
        outs.append(o)
        lses.append(lse)
    wts = jax.nn.softmax(jnp.stack(lses, axis=0), axis=0).astype(h.dtype)
    o = jnp.einsum('gblh,gblhe->blhe', wts, jnp.stack(outs, axis=0))
    return o.reshape(bsz, seq, ATTN_HEADS * ATTN_HEAD_DIM) @ w_o


def peer_ffn(h, w_q, sub_keys, u_tab, v_tab):
    bsz, seq, d = h.shape
    q = (h @ w_q).reshape(bsz, seq, PEER_HEADS, 2, PEER_HALF)
    s = jnp.einsum('blhpe,hpne->blhpn', q, sub_keys).astype(jnp.float32)
    s_top, i_top = lax.top_k(s, PEER_TOPK)
    cand = (s_top[..., 0, :, None] + s_top[..., 1, None, :]).reshape(bsz, seq, PEER_HEADS, -1)
    cand_idx = (i_top[..., 0, :, None] * PEER_N_KEYS + i_top[..., 1, None, :]).reshape(
        bsz, seq, PEER_HEADS, -1)
    best, pos = lax.top_k(cand, PEER_TOPK)
    idx = jnp.take_along_axis(cand_idx, pos, axis=-1)
    gate = jax.nn.softmax(best, axis=-1).astype(h.dtype)
    ntok = bsz * seq
    nchunk = ntok // PEER_CHUNK
    kk = PEER_HEADS * PEER_TOPK
    h_c = h.reshape(nchunk, PEER_CHUNK, d)
    i_c = idx.reshape(nchunk, PEER_CHUNK, kk)
    g_c = gate.reshape(nchunk, PEER_CHUNK, kk)

    def chunk_fn(args):
        hc, ic, gc = args
        act = jnp.einsum('cd,ckd->ck', hc, u_tab[ic])
        return jnp.einsum('ck,ckd->cd', gc * jax.nn.gelu(act), v_tab[ic])

    out = lax.map(chunk_fn, (h_c, i_c, g_c))
    return out.reshape(bsz, seq, d)


def setup_inputs(seed: int = 0) -> dict:
    key = jax.random.key(seed)
    ks = jax.random.split(key, 20)
    f32 = jnp.float32
    nrm = jax.random.normal
    nA, nB = N_SSM_LAYERS, N_ATTN_LAYERS
    G, P, H = SSM_GROUPS, SSM_STATE, SSM_GROUP
    x = nrm(ks[0], (BATCH, SEQ, D_MODEL), f32)
    norm_mix = 1.0 + 0.02 * nrm(ks[1], (DEPTH, D_MODEL), f32)
    norm_ffn = 1.0 + 0.02 * nrm(ks[2], (DEPTH, D_MODEL), f32)
    norm_final = 1.0 + 0.02 * nrm(ks[3], (D_MODEL,), f32)
    s5_lam_re = -0.5 + 0.01 * nrm(ks[4], (nA, G, P), f32)
    s5_lam_im = jnp.pi * jnp.arange(P, dtype=f32) + 0.01 * nrm(ks[5], (nA, G, P), f32)
    s5_log_step = jax.random.uniform(ks[6], (nA, G), f32,
                                     minval=math.log(STEP_MIN), maxval=math.log(STEP_MAX))
    s5_b_re = nrm(ks[7], (nA, G, P, H), f32) * (2 * H) ** -0.5
    s5_b_im = nrm(ks[8], (nA, G, P, H), f32) * (2 * H) ** -0.5
    s5_c_re = nrm(ks[9], (nA, G, H, P), f32) * (2 * P) ** -0.5
    s5_c_im = nrm(ks[10], (nA, G, H, P), f32) * (2 * P) ** -0.5
    s5_d = nrm(ks[11], (nA, D_MODEL), f32)
    s5_w_glu = nrm(ks[12], (nA, D_MODEL, 2 * D_MODEL), f32) * D_MODEL ** -0.5
    attn_w_qkv = nrm(ks[13], (nB, D_MODEL, QKV_COLS), f32) * D_MODEL ** -0.5
    attn_w_o = nrm(ks[14], (nB, ATTN_HEADS * ATTN_HEAD_DIM, D_MODEL), f32) * (
        ATTN_HEADS * ATTN_HEAD_DIM) ** -0.5
    peer_w_q = nrm(ks[15], (DEPTH, D_MODEL, PEER_HEADS * PEER_KEY_DIM), f32) * D_MODEL ** -0.5
    peer_sub_keys = nrm(ks[16], (DEPTH, PEER_HEADS, 2, PEER_N_KEYS, PEER_HALF), f32) * PEER_HALF ** -0.5
    peer_u = nrm(ks[17], (DEPTH, PEER_N_EXPERTS, D_MODEL), f32) * D_MODEL ** -0.5
    peer_v = nrm(ks[18], (DEPTH, PEER_N_EXPERTS, D_MODEL), f32) * PEER_HEADS ** -0.5
    return {"x": x, "norm_mix": norm_mix, "norm_ffn": norm_ffn, "norm_final": norm_final,
            "s5_lam_re": s5_lam_re, "s5_lam_im": s5_lam_im, "s5_log_step": s5_log_step,
            "s5_b_re": s5_b_re, "s5_b_im": s5_b_im, "s5_c_re": s5_c_re, "s5_c_im": s5_c_im,
            "s5_d": s5_d, "s5_w_glu": s5_w_glu, "attn_w_qkv": attn_w_qkv, "attn_w_o": attn_w_o,
            "peer_w_q": peer_w_q, "peer_sub_keys": peer_sub_keys, "peer_u": peer_u,
            "peer_v": peer_v}


def reference(x, norm_mix, norm_ffn, norm_final, s5_lam_re, s5_lam_im, s5_log_step,
              s5_b_re, s5_b_im, s5_c_re, s5_c_im, s5_d, s5_w_glu, attn_w_qkv, attn_w_o,
              peer_w_q, peer_sub_keys, peer_u, peer_v):
    for i in range(DEPTH):
        h = rms_norm(x, norm_mix[i])
        j = i // N_MIXERS
        if i % N_MIXERS == 0:
            mix = s5_mixer(h, s5_lam_re[j], s5_lam_im[j], s5_log_step[j], s5_b_re[j],
                           s5_b_im[j], s5_c_re[j], s5_c_im[j], s5_d[j], s5_w_glu[j])
        else:
            mix = dilated_attention_mixer(h, attn_w_qkv[j], attn_w_o[j])
        x = x + mix
        h = rms_norm(x, norm_ffn[i])
        x = x + peer_ffn(h, peer_w_q[i], peer_sub_keys[i], peer_u[i], peer_v[i])
    return rms_norm(x, norm_final)
```

```python
import functools
import math

import jax
import jax.numpy as jnp
import numpy as np
from jax import lax
from jax.experimental import pallas as pl
from jax.experimental.pallas import tpu as pltpu

F32 = jnp.float32
BF16 = jnp.bfloat16

RMS_EPS = 1e-6
SSM_GROUP = 16
SSM_STATE = 64
SSM_BLOCK_GROUPS = 16
HEAD_DIM = 64
DILATED_GROUPS = ((128, 1), (512, 4), (2048, 16))
ATTN_BLOCK = 128
ROPE_THETA = 10000.0
PEER_HEADS = 8
PEER_KEYS = 128
PEER_HALF = 128
PEER_TOPK = 16
NEG_BIG = -1e30

VMEM_LIMIT_BYTES = 56 * 1024 * 1024


def _params(*sem):
    return pltpu.CompilerParams(dimension_semantics=sem, vmem_limit_bytes=VMEM_LIMIT_BYTES)


def _rms(x, g):
    return x * lax.rsqrt(jnp.mean(x * x, axis=-1, keepdims=True) + RMS_EPS) * g


def _gelu(x):
    c = math.sqrt(2.0 / math.pi)
    return 0.5 * x * (1.0 + jnp.tanh(c * (x + 0.044715 * (x * x * x))))


def _s5_kernel(x_ref, g_ref, wbu_ref, are_ref, aim_ref, wc_ref, d_ref, y_ref, bu_ref, st_ref,
               *, batch, nblk):
    @pl.when(pl.program_id(0) == 0)
    def _():
        st_ref[...] = jnp.zeros_like(st_ref)

    rows = x_ref.shape[0]
    steps = rows // batch
    h = _rms(x_ref[...], g_ref[...])
    hb = h.astype(BF16)
    kin = wbu_ref.shape[1]
    half = wbu_ref.shape[2] // 2
    for c in range(nblk):
        bu_ref[...] = jnp.dot(hb[:, c * kin:(c + 1) * kin], wbu_ref[c],
                              preferred_element_type=F32)
        are = jnp.broadcast_to(are_ref[c], (batch, half))
        aim = jnp.broadcast_to(aim_ref[c], (batch, half))

        def step(t, carry):
            sre, sim = carry
            r = pl.multiple_of(t * batch, batch)
            bre = bu_ref[pl.ds(r, batch), 0:half]
            bim = bu_ref[pl.ds(r, batch), half:2 * half]
            nre = are * sre - aim * sim + bre
            nim = are * sim + aim * sre + bim
            bu_ref[pl.ds(r, batch), 0:half] = nre
            bu_ref[pl.ds(r, batch), half:2 * half] = nim
            return nre, nim

        sre, sim = lax.fori_loop(0, steps, step,
                                 (st_ref[c, :, 0:half], st_ref[c, :, half:2 * half]))
        st_ref[c, :, 0:half] = sre
        st_ref[c, :, half:2 * half] = sim
        yc = jnp.dot(bu_ref[...].astype(BF16), wc_ref[c], preferred_element_type=F32)
        yc = yc + d_ref[:, c * kin:(c + 1) * kin] * h[:, c * kin:(c + 1) * kin]
        y_ref[:, c * kin:(c + 1) * kin] = _gelu(yc).astype(BF16)


def _s5_weights(lam_re, lam_im, log_step, b_re, b_im, c_re, c_im):
    G, P = lam_re.shape
    H = b_re.shape[-1]
    step = jnp.exp(log_step)[:, None]
    mag = jnp.exp(lam_re * step)
    lb_re = mag * jnp.cos(lam_im * step)
    lb_im = mag * jnp.sin(lam_im * step)
    den = lam_re * lam_re + lam_im * lam_im
    num_re = lb_re - 1.0
    coef_re = (num_re * lam_re + lb_im * lam_im) / den
    coef_im = (lb_im * lam_re - num_re * lam_im) / den
    bb_re = coef_re[..., None] * b_re - coef_im[..., None] * b_im
    bb_im = coef_re[..., None] * b_im + coef_im[..., None] * b_re
    gb = SSM_BLOCK_GROUPS
    nblk = G // gb
    eye = jnp.eye(gb, dtype=F32)

    def bdiag_in(w):
        w = w.reshape(nblk, gb, P, H)
        return jnp.einsum('cgph,gk->cghkp', w, eye).reshape(nblk, gb * H, gb * P)

    def bdiag_out(w):
        w = w.reshape(nblk, gb, H, P)
        return jnp.einsum('cghp,gk->cgpkh', w, eye).reshape(nblk, gb * P, gb * H)

    wbu = jnp.concatenate([bdiag_in(bb_re), bdiag_in(bb_im)], axis=-1).astype(BF16)
    wc = jnp.concatenate([bdiag_out(c_re), -bdiag_out(c_im)], axis=1).astype(BF16)
    are = lb_re.reshape(nblk, 1, gb * P)
    aim = lb_im.reshape(nblk, 1, gb * P)
    return wbu, are, aim, wc


def _s5_call(x, g, wbu, are, aim, wc, d_skip, *, batch, steps_per_tile=64):
    n, d = x.shape
    nblk = wbu.shape[0]
    rows = batch * steps_per_tile
    assert n % rows == 0
    const3 = lambda i: (0, 0, 0)
    const2 = lambda i: (0, 0)
    return pl.pallas_call(
        functools.partial(_s5_kernel, batch=batch, nblk=nblk),
        grid=(n // rows,),
        in_specs=[pl.BlockSpec((rows, d), lambda i: (i, 0)),
                  pl.BlockSpec((1, d), const2),
                  pl.BlockSpec(wbu.shape, const3),
                  pl.BlockSpec(are.shape, const3),
                  pl.BlockSpec(aim.shape, const3),
                  pl.BlockSpec(wc.shape, const3),
                  pl.BlockSpec((1, d), const2)],
        out_specs=pl.BlockSpec((rows, d), lambda i: (i, 0)),
        out_shape=jax.ShapeDtypeStruct((n, d), BF16),
        scratch_shapes=[pltpu.VMEM((rows, wbu.shape[2]), F32),
                        pltpu.VMEM((nblk, batch, wbu.shape[2]), F32)],
        compiler_params=_params("arbitrary"),
        name="s5_ssm",
    )(x, g, wbu, are, aim, wc, d_skip)


def _glu_kernel(y_ref, x_ref, w_ref, o_ref):
    z = jnp.dot(y_ref[...], w_ref[...], preferred_element_type=F32)
    d = o_ref.shape[1]
    o_ref[...] = x_ref[...] + z[:, :d] * jax.nn.sigmoid(z[:, d:])


def _glu_call(y, x, w, *, tile=512):
    n, d = x.shape
    return pl.pallas_call(
        _glu_kernel,
        grid=(n // tile,),
        in_specs=[pl.BlockSpec((tile, d), lambda i: (i, 0)),
                  pl.BlockSpec((tile, d), lambda i: (i, 0)),
                  pl.BlockSpec(w.shape, lambda i: (0, 0))],
        out_specs=pl.BlockSpec((tile, d), lambda i: (i, 0)),
        out_shape=jax.ShapeDtypeStruct((n, d), F32),
        compiler_params=_params("parallel"),
        name="s5_glu",
    )(y, x, w)


def _rope_perm(n_heads):
    half = HEAD_DIM // 2
    idx = []
    for p in range(n_heads // 2):
        h0, h1 = 2 * p, 2 * p + 1
        for part in (0, 1):
            for h in (h0, h1):
                idx.extend(range(h * HEAD_DIM + part * half, h * HEAD_DIM + (part + 1) * half))
    return np.asarray(idx, dtype=np.int32)


def _qkv_kernel(x_ref, g_ref, w_ref, cos_ref, sin_ref, q_ref, k_ref, v_ref):
    d = x_ref.shape[1]
    hb = _rms(x_ref[...], g_ref[...]).astype(BF16)
    cos = cos_ref[...]
    sin = sin_ref[...]
    scale = HEAD_DIM ** -0.5

    def roped(col0, ref, mul):
        for c in range(d // 128):
            blk = jnp.dot(hb, w_ref[:, col0 + c * 128:col0 + (c + 1) * 128],
                          preferred_element_type=F32)
            out = blk * cos + pltpu.roll(blk, 64, axis=1) * sin
            if mul != 1.0:
                out = out * mul
            ref[:, c * 128:(c + 1) * 128] = out.astype(BF16)

    roped(0, q_ref, scale)
    roped(d, k_ref, 1.0)
    v_ref[...] = jnp.dot(hb, w_ref[:, 2 * d:3 * d], preferred_element_type=F32).astype(BF16)


def _qkv_call(x, g, w, cos, sin, *, stride, tile):
    n, d = x.shape
    m = n // stride
    tile = min(tile, m)
    xv = x.reshape(m, stride * d)
    cv = cos.reshape(m, stride * 128)
    sv = sin.reshape(m, stride * 128)
    out = jax.ShapeDtypeStruct((stride, m, d), BF16)
    ospec = pl.BlockSpec((None, tile, d), lambda c, i: (c, i, 0))
    return pl.pallas_call(
        _qkv_kernel,
        grid=(stride, m // tile),
        in_specs=[pl.BlockSpec((tile, d), lambda c, i: (i, c)),
                  pl.BlockSpec((1, d), lambda c, i: (0, 0)),
                  pl.BlockSpec(w.shape, lambda c, i: (0, 0)),
                  pl.BlockSpec((tile, 128), lambda c, i: (i, c)),
                  pl.BlockSpec((tile, 128), lambda c, i: (i, c))],
        out_specs=[ospec, ospec, ospec],
        out_shape=[out, out, out],
        compiler_params=_params("parallel", "parallel"),
        name="attn_qkv",
    )(xv, g, w, cv, sv)


def _attn_kernel(q_ref, kp_ref, kc_ref, vp_ref, vc_ref, o_ref, l_ref):
    c = q_ref.shape[0]
    d = q_ref.shape[1]
    first = pl.program_id(1) == 0
    qi = lax.broadcasted_iota(jnp.int32, (c, 2 * c), 0)
    ki = lax.broadcasted_iota(jnp.int32, (c, 2 * c), 1)
    dist = qi + c - ki
    valid = (dist >= 0) & (dist <= c) & jnp.logical_or(ki >= c, jnp.logical_not(first))
    lane = lax.broadcasted_iota(jnp.int32, (c, 128), 1)
    qmask0 = (lane // (HEAD_DIM // 2)) % 2 == 0
    omask0 = lane < HEAD_DIM
    for p in range(d // 128):
        sl = slice(p * 128, (p + 1) * 128)
        qp = q_ref[:, sl]
        kp = jnp.concatenate([kp_ref[:, sl], kc_ref[:, sl]], axis=0)
        vp = jnp.concatenate([vp_ref[:, sl], vc_ref[:, sl]], axis=0)
        outs, lses = [], []
        for e in range(2):
            qm = jnp.where(qmask0 if e == 0 else jnp.logical_not(qmask0), qp, jnp.zeros_like(qp))
            s = lax.dot_general(qm, kp, (((1,), (1,)), ((), ())), preferred_element_type=F32)
            s = jnp.where(valid, s, NEG_BIG)
            smax = jnp.max(s, axis=-1, keepdims=True)
            ex = jnp.exp(s - smax)
            den = jnp.sum(ex, axis=-1, keepdims=True)
            o = jnp.dot(ex.astype(BF16), vp, preferred_element_type=F32)
            outs.append(o / den)
            lses.append(jnp.broadcast_to(smax + jnp.log(den), (c, 128)))
        o_ref[:, sl] = jnp.where(omask0, outs[0], outs[1]).astype(BF16)
        l_ref[:, sl] = jnp.where(omask0, lses[0], lses[1])


def _attn_call(q, k, v):
    stride, m, d = q.shape
    c = ATTN_BLOCK
    nb = m // c
    cur = pl.BlockSpec((None, c, d), lambda s, i: (s, i, 0))
    prev = pl.BlockSpec((None, c, d), lambda s, i: (s, jnp.maximum(i - 1, 0), 0))
    ospec = pl.BlockSpec((c, d), lambda s, i: (i, s))
    o, lse = pl.pallas_call(
        _attn_kernel,
        grid=(stride, nb),
        in_specs=[cur, prev, cur, prev, cur],
        out_specs=[ospec, ospec],
        out_shape=[jax.ShapeDtypeStruct((m, stride * d), BF16),
                   jax.ShapeDtypeStruct((m, stride * d), F32)],
        compiler_params=_params("parallel", "parallel"),
        name="attn_band",
    )(q, k, k, v, v)
    return o.reshape(m * stride, d), lse.reshape(m * stride, d)


def _attn_out_kernel(x_ref, o0, o1, o2, l0, l1, l2, w_ref, y_ref):
    a, b, c = l0[...], l1[...], l2[...]
    mx = jnp.maximum(jnp.maximum(a, b), c)
    ea, eb, ec = jnp.exp(a - mx), jnp.exp(b - mx), jnp.exp(c - mx)
    inv = 1.0 / (ea + eb + ec)
    o = (ea * inv) * o0[...].astype(F32) + (eb * inv) * o1[...].astype(F32) \
        + (ec * inv) * o2[...].astype(F32)
    y_ref[...] = x_ref[...] + jnp.dot(o.astype(BF16), w_ref[...], preferred_element_type=F32)


def _attn_out_call(x, os_, ls_, w, *, tile=512):
    n, d = x.shape
    row = pl.BlockSpec((tile, d), lambda i: (i, 0))
    return pl.pallas_call(
        _attn_out_kernel,
        grid=(n // tile,),
        in_specs=[row] * 7 + [pl.BlockSpec(w.shape, lambda i: (0, 0))],
        out_specs=row,
        out_shape=jax.ShapeDtypeStruct((n, d), F32),
        compiler_params=_params("parallel"),
        name="attn_out",
    )(x, *os_, *ls_, w)


def _topk_rank(s):
    nk, t = s.shape
    iota = lax.broadcasted_iota(jnp.int32, (nk, t), 0)
    row16 = lax.broadcasted_iota(jnp.int32, (PEER_TOPK, t), 0)
    rank = jnp.full((nk, t), 99, jnp.int32)
    vals = jnp.zeros((PEER_TOPK, t), F32)
    for k in range(PEER_TOPK):
        m = jnp.max(s, axis=0, keepdims=True)
        idx = jnp.min(jnp.where(s == m, iota, nk), axis=0, keepdims=True)
        sel = iota == idx
        rank = jnp.where(sel, k, rank)
        s = jnp.where(sel, -jnp.inf, s)
        vals = jnp.where(row16 == k, m, vals)
    return rank, vals


_CAND_BLOCKS = ((0, 0, 8), (0, 8, 8), (1, 0, 8), (2, 0, 5), (3, 0, 4), (4, 0, 3), (5, 0, 2),
                (6, 0, 2), (7, 0, 2))


def _pair_counts(a, b):
    t = a.shape[1]
    sub = lax.broadcasted_iota(jnp.int32, (8, t), 0)
    cands, flats = [], []
    for k, l0, cnt in _CAND_BLOCKS:
        cnd = a[k:k + 1, :] + b[l0:l0 + 8, :]
        if cnt < 8:
            cnd = jnp.where(sub < cnt, cnd, -jnp.inf)
        cands.append(cnd)
        flats.append(sub + (k * PEER_TOPK + l0))
    cands.append(a[8:16, :] + b[0:1, :])
    flats.append((sub + 8) * PEER_TOPK)
    top = a[0:1, :] + b[0:1, :]
    sels = [jnp.zeros((8, t), jnp.bool_) for _ in cands]
    zsum = jnp.zeros((1, t), F32)
    big = PEER_TOPK * PEER_TOPK
    for _ in range(PEER_TOPK):
        m = functools.reduce(jnp.maximum, cands)
        m = jnp.max(m, axis=0, keepdims=True)
        idx = functools.reduce(jnp.minimum,
                               [jnp.where(c == m, f, big) for c, f in zip(cands, flats)])
        idx = jnp.min(idx, axis=0, keepdims=True)
        hit = [f == idx for f in flats]
        cands = [jnp.where(h_, -jnp.inf, c) for h_, c in zip(hit, cands)]
        sels = [jnp.logical_or(s_, h_) for s_, h_ in zip(sels, hit)]
        zsum = zsum + jnp.exp(m - top)
    row16 = lax.broadcasted_iota(jnp.int32, (PEER_TOPK, t), 0)
    selfs = [s_.astype(F32) for s_ in sels]
    per_k = [jnp.sum(selfs[0] + selfs[1], axis=0, keepdims=True)]
    per_k += [jnp.sum(s_, axis=0, keepdims=True) for s_ in selfs[2:9]]
    counts = jnp.concatenate([jnp.zeros((8, t), F32), selfs[9]], axis=0)
    for k in range(8):
        counts = jnp.where(row16 == k, per_k[k], counts)
    return counts, zsum


def _route_kernel(x_ref, g_ref, wq_ref, sk_ref, hb_ref, r2_ref, e2_ref, nn_ref, p_ref):
    hb = _rms(x_ref[...], g_ref[...]).astype(BF16)
    hb_ref[...] = hb
    qt = lax.dot_general(wq_ref[...], hb, (((1,), (1,)), ((), ())),
                         preferred_element_type=F32).astype(BF16)
    for h in range(PEER_HEADS):
        base = 2 * h * PEER_HALF
        s1 = jnp.dot(sk_ref[2 * h], qt[base:base + PEER_HALF], preferred_element_type=F32)
        s2 = jnp.dot(sk_ref[2 * h + 1], qt[base + PEER_HALF:base + 2 * PEER_HALF],
                     preferred_element_type=F32)
        rank1, a = _topk_rank(s1)
        rank2, b = _topk_rank(s2)
        counts, zsum = _pair_counts(a, b)
        nn = jnp.zeros(s1.shape, F32)
        for k in range(PEER_TOPK):
            nn = jnp.where(rank1 == k, counts[k:k + 1, :], nn)
        r2_ref[h] = rank2.astype(F32).astype(BF16)
        e2_ref[h] = jnp.exp(s2 - b[0:1, :]).astype(BF16)
        nn_ref[h] = nn
        p_ref[h] = jnp.exp(s1 - a[0:1, :]) / zsum


def _route_call(x, g, wq_t, sk, *, tile=256):
    n, d = x.shape
    hk = (PEER_HEADS, PEER_KEYS, n)
    blk = pl.BlockSpec((PEER_HEADS, PEER_KEYS, tile), lambda i: (0, 0, i))
    return pl.pallas_call(
        _route_kernel,
        grid=(n // tile,),
        in_specs=[pl.BlockSpec((tile, d), lambda i: (i, 0)),
                  pl.BlockSpec((1, d), lambda i: (0, 0)),
                  pl.BlockSpec(wq_t.shape, lambda i: (0, 0)),
                  pl.BlockSpec(sk.shape, lambda i: (0, 0, 0))],
        out_specs=[pl.BlockSpec((tile, d), lambda i: (i, 0)), blk, blk, blk, blk],
        out_shape=[jax.ShapeDtypeStruct((n, d), BF16),
                   jax.ShapeDtypeStruct(hk, BF16), jax.ShapeDtypeStruct(hk, BF16),
                   jax.ShapeDtypeStruct(hk, F32), jax.ShapeDtypeStruct(hk, F32)],
        compiler_params=_params("parallel"),
        name="peer_route",
    )(x, g, wq_t, sk)


def _peer_kernel(x_ref, hb_ref, u_ref, vt_ref, r2_ref, e2_ref, nn_ref, p_ref, gf_ref, o_ref,
                 act_ref, w_ref, acc_ref, *, final_norm):
    et = pl.program_id(1)
    te = u_ref.shape[0]
    tm = hb_ref.shape[0]
    rows_i = te // PEER_KEYS

    @pl.when(et == 0)
    def _():
        acc_ref[...] = jnp.zeros_like(acc_ref)

    act_ref[...] = lax.dot_general(u_ref[...], hb_ref[...], (((1,), (1,)), ((), ())),
                                   preferred_element_type=F32)
    for ii in range(rows_i):
        i = et * rows_i + ii
        gate = jnp.zeros((PEER_KEYS, tm), BF16)
        for h in range(PEER_HEADS):
            nn = jnp.broadcast_to(nn_ref[h, pl.ds(i, 1), :], (PEER_KEYS, tm)).astype(BF16)
            p = jnp.broadcast_to(p_ref[h, pl.ds(i, 1), :], (PEER_KEYS, tm)).astype(BF16)
            gate = gate + jnp.where(r2_ref[h] < nn, e2_ref[h] * p, jnp.zeros_like(gate))
        a = _gelu(act_ref[ii * PEER_KEYS:(ii + 1) * PEER_KEYS, :])
        w_ref[ii * PEER_KEYS:(ii + 1) * PEER_KEYS, :] = (gate.astype(F32) * a).astype(BF16)
    acc_ref[...] += jnp.dot(vt_ref[...], w_ref[...], preferred_element_type=F32)

    @pl.when(et == pl.num_programs(1) - 1)
    def _():
        y = x_ref[...] + acc_ref[...].T
        if final_norm:
            y = _rms(y, gf_ref[...])
        o_ref[...] = y


def _peer_call(x, hb, u, vt, r2, e2, nn, p, g_final, *, final_norm, tm=512, te=1024):
    n, d = x.shape
    ne = u.shape[0]
    tok = pl.BlockSpec((tm, d), lambda t, e: (t, 0))
    route = pl.BlockSpec((PEER_HEADS, PEER_KEYS, tm), lambda t, e: (0, 0, t))
    return pl.pallas_call(
        functools.partial(_peer_kernel, final_norm=final_norm),
        grid=(n // tm, ne // te),
        in_specs=[tok, tok,
                  pl.BlockSpec((te, d), lambda t, e: (e, 0)),
                  pl.BlockSpec((d, te), lambda t, e: (0, e)),
                  route, route, route, route,
                  pl.BlockSpec((1, d), lambda t, e: (0, 0))],
        out_specs=tok,
        out_shape=jax.ShapeDtypeStruct((n, d), F32),
        scratch_shapes=[pltpu.VMEM((te, tm), F32), pltpu.VMEM((te, tm), BF16),
                        pltpu.VMEM((d, tm), F32)],
        compiler_params=_params("parallel", "arbitrary"),
        name="peer_dense",
    )(x, hb, u, vt, r2, e2, nn, p, g_final)


def _peer_layer(x, g, w_q, sub_keys, u_tab, v_tab, g_final, final_norm):
    wq_t = w_q.T.astype(BF16)
    sk = sub_keys.reshape(PEER_HEADS * 2, PEER_KEYS, PEER_HALF).astype(BF16)
    hb, r2, e2, nn, p = _route_call(x, g, wq_t, sk)
    return _peer_call(x, hb, u_tab.astype(BF16), v_tab.T.astype(BF16), r2, e2, nn, p, g_final,
                      final_norm=final_norm)


def _rope_tables(seq, batch):
    half = HEAD_DIM // 2
    inv = ROPE_THETA ** (-jnp.arange(half, dtype=F32) / half)
    ang = jnp.arange(seq, dtype=F32)[:, None] * inv[None, :]
    cos = jnp.tile(jnp.cos(ang), (1, 4))
    sin = jnp.sin(ang)
    sin = jnp.concatenate([-sin, -sin, sin, sin], axis=1)
    rep = lambda t: jnp.broadcast_to(t[:, None, :], (seq, batch, 128)).reshape(seq * batch, 128)
    return rep(cos), rep(sin)


def _attention_layer(x, g, w_qkv, w_o, *, batch, seq):
    n, d = x.shape
    n_heads = d // HEAD_DIM
    perm = _rope_perm(n_heads)
    cos, sin = _rope_tables(seq, batch)
    wg = w_qkv.reshape(d, len(DILATED_GROUPS), 3, d)
    outs, lses = [], []
    for gi, (window, dilation) in enumerate(DILATED_GROUPS):
        assert window // dilation == ATTN_BLOCK and (seq // dilation) % ATTN_BLOCK == 0
        w = jnp.concatenate([wg[:, gi, 0][:, perm], wg[:, gi, 1][:, perm], wg[:, gi, 2]],
                            axis=1).astype(BF16)
        q, k, v = _qkv_call(x, g, w, cos, sin, stride=dilation * batch, tile=512)
        o, lse = _attn_call(q, k, v)
        outs.append(o)
        lses.append(lse)
    return _attn_out_call(x, outs, lses, w_o.astype(BF16))


def kernel(x, norm_mix, norm_ffn, norm_final, s5_lam_re, s5_lam_im, s5_log_step, s5_b_re, s5_b_im, s5_c_re, s5_c_im, s5_d, s5_w_glu, attn_w_qkv, attn_w_o, peer_w_q, peer_sub_keys, peer_u, peer_v):
    batch, seq, d = x.shape
    assert batch == 8, "one timestep of all batches must fill one 8-sublane group"
    depth = norm_mix.shape[0]
    xs = x.transpose(1, 0, 2).reshape(seq * batch, d)
    g_final = norm_final.reshape(1, d)
    for i in range(depth):
        j = i // 2
        g_mix = norm_mix[i].reshape(1, d)
        if i % 2 == 0:
            wbu, are, aim, wc = _s5_weights(s5_lam_re[j], s5_lam_im[j], s5_log_step[j],
                                            s5_b_re[j], s5_b_im[j], s5_c_re[j], s5_c_im[j])
            y = _s5_call(xs, g_mix, wbu, are, aim, wc, s5_d[j].reshape(1, d), batch=batch)
            xs = _glu_call(y, xs, s5_w_glu[j].astype(BF16))
        else:
            xs = _attention_layer(xs, g_mix, attn_w_qkv[j], attn_w_o[j], batch=batch, seq=seq)
        xs = _peer_layer(xs, norm_ffn[i].reshape(1, d), peer_w_q[i], peer_sub_keys[i],
                         peer_u[i], peer_v[i], g_final, final_norm=(i == depth - 1))
    return xs.reshape(seq, batch, d).transpose(1, 0, 2)
```

```python
import functools
import math

import jax
import jax.numpy as jnp
import numpy as np
from jax import lax
from jax.experimental import pallas as pl
from jax.experimental.pallas import tpu as pltpu

F32 = jnp.float32
BF16 = jnp.bfloat16

RMS_EPS = 1e-6
SSM_GROUP = 16
SSM_STATE = 64
SSM_BLOCK_GROUPS = 16
HEAD_DIM = 64
DILATED_GROUPS = ((128, 1), (512, 4), (2048, 16))
ATTN_BLOCK = 128
ROPE_THETA = 10000.0
PEER_HEADS = 8
PEER_KEYS = 128
PEER_HALF = 128
PEER_TOPK = 16
NEG_BIG = -1e30

VMEM_LIMIT_BYTES = 56 * 1024 * 1024


def _params(*sem):
    return pltpu.CompilerParams(dimension_semantics=sem, vmem_limit_bytes=VMEM_LIMIT_BYTES)


def _rms(x, g):
    return x * lax.rsqrt(jnp.mean(x * x, axis=-1, keepdims=True) + RMS_EPS) * g


def _gelu(x):
    c = math.sqrt(2.0 / math.pi)
    return 0.5 * x * (1.0 + jnp.tanh(c * (x + 0.044715 * (x * x * x))))


def _s5_kernel(x_ref, g_ref, wbu_ref, are_ref, aim_ref, wc_ref, d_ref, y_ref, bu_ref, st_ref,
               *, batch, nblk):
    @pl.when(pl.program_id(0) == 0)
    def _():
        st_ref[...] = jnp.zeros_like(st_ref)

    rows = x_ref.shape[0]
    steps = rows // batch
    h = _rms(x_ref[...], g_ref[...])
    hb = h.astype(BF16)
    kin = wbu_ref.shape[1]
    half = wbu_ref.shape[2] // 2
    for c in range(nblk):
        bu_ref[...] = jnp.dot(hb[:, c * kin:(c + 1) * kin], wbu_ref[c],
                              preferred_element_type=F32)
        are = jnp.broadcast_to(are_ref[c], (batch, half))
        aim = jnp.broadcast_to(aim_ref[c], (batch, half))

        def step(t, carry):
            sre, sim = carry
            r = pl.multiple_of(t * batch, batch)
            bre = bu_ref[pl.ds(r, batch), 0:half]
            bim = bu_ref[pl.ds(r, batch), half:2 * half]
            nre = are * sre - aim * sim + bre
            nim = are * sim + aim * sre + bim
            bu_ref[pl.ds(r, batch), 0:half] = nre
            bu_ref[pl.ds(r, batch), half:2 * half] = nim
            return nre, nim

        sre, sim = lax.fori_loop(0, steps, step,
                                 (st_ref[c, :, 0:half], st_ref[c, :, half:2 * half]))
        st_ref[c, :, 0:half] = sre
        st_ref[c, :, half:2 * half] = sim
        yc = jnp.dot(bu_ref[...].astype(BF16), wc_ref[c], preferred_element_type=F32)
        yc = yc + d_ref[:, c * kin:(c + 1) * kin] * h[:, c * kin:(c + 1) * kin]
        y_ref[:, c * kin:(c + 1) * kin] = _gelu(yc).astype(BF16)


def _s5_weights(lam_re, lam_im, log_step, b_re, b_im, c_re, c_im):
    G, P = lam_re.shape
    H = b_re.shape[-1]
    step = jnp.exp(log_step)[:, None]
    mag = jnp.exp(lam_re * step)
    lb_re = mag * jnp.cos(lam_im * step)
    lb_im = mag * jnp.sin(lam_im * step)
    den = lam_re * lam_re + lam_im * lam_im
    num_re = lb_re - 1.0
    coef_re = (num_re * lam_re + lb_im * lam_im) / den
    coef_im = (lb_im * lam_re - num_re * lam_im) / den
    bb_re = coef_re[..., None] * b_re - coef_im[..., None] * b_im
    bb_im = coef_re[..., None] * b_im + coef_im[..., None] * b_re
    gb = SSM_BLOCK_GROUPS
    nblk = G // gb
    eye = jnp.eye(gb, dtype=F32)

    def bdiag_in(w):
        w = w.reshape(nblk, gb, P, H)
        return jnp.einsum('cgph,gk->cghkp', w, eye).reshape(nblk, gb * H, gb * P)

    def bdiag_out(w):
        w = w.reshape(nblk, gb, H, P)
        return jnp.einsum('cghp,gk->cgpkh', w, eye).reshape(nblk, gb * P, gb * H)

    wbu = jnp.concatenate([bdiag_in(bb_re), bdiag_in(bb_im)], axis=-1).astype(BF16)
    wc = jnp.concatenate([bdiag_out(c_re), -bdiag_out(c_im)], axis=1).astype(BF16)
    are = lb_re.reshape(nblk, 1, gb * P)
    aim = lb_im.reshape(nblk, 1, gb * P)
    return wbu, are, aim, wc


def _s5_call(x, g, wbu, are, aim, wc, d_skip, *, batch, steps_per_tile=64):
    n, d = x.shape
    nblk = wbu.shape[0]
    rows = batch * steps_per_tile
    assert n % rows == 0
    const3 = lambda i: (0, 0, 0)
    const2 = lambda i: (0, 0)
    return pl.pallas_call(
        functools.partial(_s5_kernel, batch=batch, nblk=nblk),
        grid=(n // rows,),
        in_specs=[pl.BlockSpec((rows, d), lambda i: (i, 0)),
                  pl.BlockSpec((1, d), const2),
                  pl.BlockSpec(wbu.shape, const3),
                  pl.BlockSpec(are.shape, const3),
                  pl.BlockSpec(aim.shape, const3),
                  pl.BlockSpec(wc.shape, const3),
                  pl.BlockSpec((1, d), const2)],
        out_specs=pl.BlockSpec((rows, d), lambda i: (i, 0)),
        out_shape=jax.ShapeDtypeStruct((n, d), BF16),
        scratch_shapes=[pltpu.VMEM((rows, wbu.shape[2]), F32),
                        pltpu.VMEM((nblk, batch, wbu.shape[2]), F32)],
        compiler_params=_params("arbitrary"),
        name="s5_ssm",
    )(x, g, wbu, are, aim, wc, d_skip)


def _glu_kernel(y_ref, x_ref, w_ref, o_ref):
    z = jnp.dot(y_ref[...], w_ref[...], preferred_element_type=F32)
    d = o_ref.shape[1]
    o_ref[...] = x_ref[...] + z[:, :d] * jax.nn.sigmoid(z[:, d:])


def _glu_call(y, x, w, *, tile=512):
    n, d = x.shape
    return pl.pallas_call(
        _glu_kernel,
        grid=(n // tile,),
        in_specs=[pl.BlockSpec((tile, d), lambda i: (i, 0)),
                  pl.BlockSpec((tile, d), lambda i: (i, 0)),
                  pl.BlockSpec(w.shape, lambda i: (0, 0))],
        out_specs=pl.BlockSpec((tile, d), lambda i: (i, 0)),
        out_shape=jax.ShapeDtypeStruct((n, d), F32),
        compiler_params=_params("parallel"),
        name="s5_glu",
    )(y, x, w)


def _rope_perm(n_heads):
    half = HEAD_DIM // 2
    idx = []
    for p in range(n_heads // 2):
        h0, h1 = 2 * p, 2 * p + 1
        for part in (0, 1):
            for h in (h0, h1):
                idx.extend(range(h * HEAD_DIM + part * half, h * HEAD_DIM + (part + 1) * half))
    return np.asarray(idx, dtype=np.int32)


def _qkv_kernel(x_ref, g_ref, w_ref, cos_ref, sin_ref, q_ref, k_ref, v_ref):
    d = x_ref.shape[1]
    hb = _rms(x_ref[...], g_ref[...]).astype(BF16)
    cos = cos_ref[...]
    sin = sin_ref[...]
    scale = HEAD_DIM ** -0.5

    def roped(col0, ref, mul):
        for c in range(d // 128):
            blk = jnp.dot(hb, w_ref[:, col0 + c * 128:col0 + (c + 1) * 128],
                          preferred_element_type=F32)
            out = blk * cos + pltpu.roll(blk, 64, axis=1) * sin
            if mul != 1.0:
                out = out * mul
            ref[:, c * 128:(c + 1) * 128] = out.astype(BF16)

    roped(0, q_ref, scale)
    roped(d, k_ref, 1.0)
    v_ref[...] = jnp.dot(hb, w_ref[:, 2 * d:3 * d], preferred_element_type=F32).astype(BF16)


def _qkv_call(x, g, w, cos, sin, *, stride, tile):
    n, d = x.shape
    m = n // stride
    tile = min(tile, m)
    xv = x.reshape(m, stride * d)
    cv = cos.reshape(m, stride * 128)
    sv = sin.reshape(m, stride * 128)
    out = jax.ShapeDtypeStruct((stride, m, d), BF16)
    ospec = pl.BlockSpec((None, tile, d), lambda c, i: (c, i, 0))
    return pl.pallas_call(
        _qkv_kernel,
        grid=(stride, m // tile),
        in_specs=[pl.BlockSpec((tile, d), lambda c, i: (i, c)),
                  pl.BlockSpec((1, d), lambda c, i: (0, 0)),
                  pl.BlockSpec(w.shape, lambda c, i: (0, 0)),
                  pl.BlockSpec((tile, 128), lambda c, i: (i, c)),
                  pl.BlockSpec((tile, 128), lambda c, i: (i, c))],
        out_specs=[ospec, ospec, ospec],
        out_shape=[out, out, out],
        compiler_params=_params("parallel", "parallel"),
        name="attn_qkv",
    )(xv, g, w, cv, sv)


def _attn_kernel(q_ref, kp_ref, kc_ref, vp_ref, vc_ref, o_ref, l_ref):
    c = q_ref.shape[0]
    d = q_ref.shape[1]
    first = pl.program_id(1) == 0
    qi = lax.broadcasted_iota(jnp.int32, (c, 2 * c), 0)
    ki = lax.broadcasted_iota(jnp.int32, (c, 2 * c), 1)
    dist = qi + c - ki
    valid = (dist >= 0) & (dist <= c) & jnp.logical_or(ki >= c, jnp.logical_not(first))
    lane = lax.broadcasted_iota(jnp.int32, (c, 128), 1)
    qmask0 = (lane // (HEAD_DIM // 2)) % 2 == 0
    omask0 = lane < HEAD_DIM
    for p in range(d // 128):
        sl = slice(p * 128, (p + 1) * 128)
        qp = q_ref[:, sl]
        kp = jnp.concatenate([kp_ref[:, sl], kc_ref[:, sl]], axis=0)
        vp = jnp.concatenate([vp_ref[:, sl], vc_ref[:, sl]], axis=0)
        outs, lses = [], []
        for e in range(2):
            qm = jnp.where(qmask0 if e == 0 else jnp.logical_not(qmask0), qp, jnp.zeros_like(qp))
            s = lax.dot_general(qm, kp, (((1,), (1,)), ((), ())), preferred_element_type=F32)
            s = jnp.where(valid, s, NEG_BIG)
            smax = jnp.max(s, axis=-1, keepdims=True)
            ex = jnp.exp(s - smax)
            den = jnp.sum(ex, axis=-1, keepdims=True)
            o = jnp.dot(ex.astype(BF16), vp, preferred_element_type=F32)
            outs.append(o / den)
            lses.append(jnp.broadcast_to(smax + jnp.log(den), (c, 128)))
        o_ref[:, sl] = jnp.where(omask0, outs[0], outs[1]).astype(BF16)
        l_ref[:, sl] = jnp.where(omask0, lses[0], lses[1])


def _attn_call(q, k, v):
    stride, m, d = q.shape
    c = ATTN_BLOCK
    nb = m // c
    cur = pl.BlockSpec((None, c, d), lambda s, i: (s, i, 0))
    prev = pl.BlockSpec((None, c, d), lambda s, i: (s, jnp.maximum(i - 1, 0), 0))
    ospec = pl.BlockSpec((c, d), lambda s, i: (i, s))
    o, lse = pl.pallas_call(
        _attn_kernel,
        grid=(stride, nb),
        in_specs=[cur, prev, cur, prev, cur],
        out_specs=[ospec, ospec],
        out_shape=[jax.ShapeDtypeStruct((m, stride * d), BF16),
                   jax.ShapeDtypeStruct((m, stride * d), F32)],
        compiler_params=_params("parallel", "parallel"),
        name="attn_band",
    )(q, k, k, v, v)
    return o.reshape(m * stride, d), lse.reshape(m * stride, d)


def _attn_out_kernel(x_ref, o0, o1, o2, l0, l1, l2, w_ref, y_ref):
    a, b, c = l0[...], l1[...], l2[...]
    mx = jnp.maximum(jnp.maximum(a, b), c)
    ea, eb, ec = jnp.exp(a - mx), jnp.exp(b - mx), jnp.exp(c - mx)
    inv = 1.0 / (ea + eb + ec)
    o = (ea * inv) * o0[...].astype(F32) + (eb * inv) * o1[...].astype(F32) \
        + (ec * inv) * o2[...].astype(F32)
    y_ref[...] = x_ref[...] + jnp.dot(o.astype(BF16), w_ref[...], preferred_element_type=F32)


def _attn_out_call(x, os_, ls_, w, *, tile=512):
    n, d = x.shape
    row = pl.BlockSpec((tile, d), lambda i: (i, 0))
    return pl.pallas_call(
        _attn_out_kernel,
        grid=(n // tile,),
        in_specs=[row] * 7 + [pl.BlockSpec(w.shape, lambda i: (0, 0))],
        out_specs=row,
        out_shape=jax.ShapeDtypeStruct((n, d), F32),
        compiler_params=_params("parallel"),
        name="attn_out",
    )(x, *os_, *ls_, w)


def _topk_rank(s):
    nk, t = s.shape
    iota = lax.broadcasted_iota(jnp.int32, (nk, t), 0)
    row16 = lax.broadcasted_iota(jnp.int32, (PEER_TOPK, t), 0)
    rank = jnp.full((nk, t), 99, jnp.int32)
    vals = jnp.zeros((PEER_TOPK, t), F32)
    for k in range(PEER_TOPK):
        m = jnp.max(s, axis=0, keepdims=True)
        idx = jnp.min(jnp.where(s == m, iota, nk), axis=0, keepdims=True)
        sel = iota == idx
        rank = jnp.where(sel, k, rank)
        s = jnp.where(sel, -jnp.inf, s)
        vals = jnp.where(row16 == k, m, vals)
    return rank, vals


SENTINEL_BASE = 1e38
SENTINEL_STEP = 1e37
SENTINEL_LIMIT = -0.95e38


def _topk_rank_distinct(s):
    nk, t = s.shape
    row16 = lax.broadcasted_iota(jnp.int32, (PEER_TOPK, t), 0)
    low = jnp.min(s, axis=0, keepdims=True)
    vals = jnp.zeros((PEER_TOPK, t), F32)
    for k in range(PEER_TOPK):
        m = jnp.max(s, axis=0, keepdims=True)
        s = jnp.where(s == m, -(SENTINEL_BASE + k * SENTINEL_STEP), s)
        vals = jnp.where(row16 == k, m, vals)
    top = s < SENTINEL_LIMIT
    rank = jnp.where(top, jnp.floor(s * (-1.0 / SENTINEL_STEP) - (SENTINEL_BASE / SENTINEL_STEP - 0.5)),
                     99.0)
    cnt = jnp.sum(top.astype(F32), axis=0, keepdims=True)
    bad = jnp.logical_or(cnt != float(PEER_TOPK), jnp.logical_not(low > SENTINEL_LIMIT))
    return rank, vals, bad.astype(jnp.int32)


_CAND_BLOCKS = ((0, 0, 8), (0, 8, 8), (1, 0, 8), (2, 0, 5), (3, 0, 4), (4, 0, 3), (5, 0, 2),
                (6, 0, 2), (7, 0, 2))


def _candidates(a, b):
    t = a.shape[1]
    sub = lax.broadcasted_iota(jnp.int32, (8, t), 0)
    cands, flats, valid = [], [], []
    for k, l0, cnt in _CAND_BLOCKS:
        cnd = a[k:k + 1, :] + b[l0:l0 + 8, :]
        if cnt < 8:
            cnd = jnp.where(sub < cnt, cnd, -jnp.inf)
        cands.append(cnd)
        flats.append(sub + (k * PEER_TOPK + l0))
        valid.append(None if cnt == 8 else sub < cnt)
    cands.append(a[8:16, :] + b[0:1, :])
    flats.append((sub + 8) * PEER_TOPK)
    valid.append(None)
    return cands, flats, valid


def _counts_from_sels(sels):
    t = sels[0].shape[1]
    row16 = lax.broadcasted_iota(jnp.int32, (PEER_TOPK, t), 0)
    selfs = [s_.astype(F32) for s_ in sels]
    per_k = [jnp.sum(selfs[0] + selfs[1], axis=0, keepdims=True)]
    per_k += [jnp.sum(s_, axis=0, keepdims=True) for s_ in selfs[2:9]]
    counts = jnp.concatenate([jnp.zeros((8, t), F32), selfs[9]], axis=0)
    for k in range(8):
        counts = jnp.where(row16 == k, per_k[k], counts)
    return counts


def _pair_counts(a, b):
    t = a.shape[1]
    cands, flats, _ = _candidates(a, b)
    top = a[0:1, :] + b[0:1, :]
    sels = [jnp.zeros((8, t), jnp.bool_) for _ in cands]
    zsum = jnp.zeros((1, t), F32)
    big = PEER_TOPK * PEER_TOPK
    for _ in range(PEER_TOPK):
        m = functools.reduce(jnp.maximum, cands)
        m = jnp.max(m, axis=0, keepdims=True)
        idx = functools.reduce(jnp.minimum,
                               [jnp.where(c == m, f, big) for c, f in zip(cands, flats)])
        idx = jnp.min(idx, axis=0, keepdims=True)
        hit = [f == idx for f in flats]
        cands = [jnp.where(h_, -jnp.inf, c) for h_, c in zip(hit, cands)]
        sels = [jnp.logical_or(s_, h_) for s_, h_ in zip(sels, hit)]
        zsum = zsum + jnp.exp(m - top)
    return _counts_from_sels(sels), zsum


def _pair_counts_distinct(a, b):
    cands, _, valid = _candidates(a, b)
    top = a[0:1, :] + b[0:1, :]
    zsum = jnp.zeros((1, a.shape[1]), F32)
    for _ in range(PEER_TOPK):
        m = functools.reduce(jnp.maximum, cands)
        m = jnp.max(m, axis=0, keepdims=True)
        cands = [jnp.where(c == m, -jnp.inf, c) for c in cands]
        zsum = zsum + jnp.exp(m - top)
    sels = [c == -jnp.inf if v is None else jnp.logical_and(c == -jnp.inf, v)
            for c, v in zip(cands, valid)]
    counts = _counts_from_sels(sels)
    total = jnp.sum(counts, axis=0, keepdims=True)
    return counts, zsum, (total != float(PEER_TOPK)).astype(jnp.int32)


def _route_head(s1, s2, exact):
    if exact:
        rank1, a = _topk_rank(s1)
        rank2, b = _topk_rank(s2)
        counts, zsum = _pair_counts(a, b)
        bad = None
    else:
        rank1, a, bad1 = _topk_rank_distinct(s1)
        rank2, b, bad2 = _topk_rank_distinct(s2)
        counts, zsum, bad3 = _pair_counts_distinct(a, b)
        bad = jnp.max(bad1 + bad2 + bad3)
    nn = jnp.zeros(s1.shape, F32)
    for k in range(PEER_TOPK):
        nn = jnp.where(rank1 == k, counts[k:k + 1, :], nn)
    r2 = rank2.astype(F32).astype(BF16)
    e2 = jnp.exp(s2 - b[0:1, :]).astype(BF16)
    p = jnp.exp(s1 - a[0:1, :]) / zsum
    return (r2, e2, nn, p), bad


def _route_kernel(x_ref, g_ref, wq_ref, sk_ref, hb_ref, r2_ref, e2_ref, nn_ref, p_ref, sc_ref):
    hb = _rms(x_ref[...], g_ref[...]).astype(BF16)
    hb_ref[...] = hb
    qt = lax.dot_general(wq_ref[...], hb, (((1,), (1,)), ((), ())),
                         preferred_element_type=F32).astype(BF16)
    for hp in range(2 * PEER_HEADS):
        sc_ref[hp] = jnp.dot(sk_ref[hp], qt[hp * PEER_HALF:(hp + 1) * PEER_HALF],
                             preferred_element_type=F32)
    slabs = nn_ref.shape[1]

    def head(h, carry):
        s1 = sc_ref[2 * h]
        s2 = sc_ref[2 * h + 1]

        def store(vals):
            r2, e2, nn, p = vals
            r2_ref[h] = r2
            e2_ref[h] = e2
            for c in range(slabs):
                nn_ref[h, c] = nn[:, c * 128:(c + 1) * 128]
                p_ref[h, c] = p[:, c * 128:(c + 1) * 128]

        vals, bad = _route_head(s1, s2, exact=False)
        store(vals)

        @pl.when(bad > 0)
        def _():
            store(_route_head(s1, s2, exact=True)[0])

        return carry

    lax.fori_loop(0, PEER_HEADS, head, 0)


def _route_call(x, g, wq_t, sk, *, tile=512):
    n, d = x.shape
    hk = (PEER_HEADS, PEER_KEYS, n)
    hs = (PEER_HEADS, n // 128, PEER_KEYS, 128)
    blk = pl.BlockSpec((PEER_HEADS, PEER_KEYS, tile), lambda i: (0, 0, i))
    slab = pl.BlockSpec((PEER_HEADS, tile // 128, PEER_KEYS, 128), lambda i: (0, i, 0, 0))
    return pl.pallas_call(
        _route_kernel,
        grid=(n // tile,),
        in_specs=[pl.BlockSpec((tile, d), lambda i: (i, 0)),
                  pl.BlockSpec((1, d), lambda i: (0, 0)),
                  pl.BlockSpec(wq_t.shape, lambda i: (0, 0)),
                  pl.BlockSpec(sk.shape, lambda i: (0, 0, 0))],
        out_specs=[pl.BlockSpec((tile, d), lambda i: (i, 0)), blk, blk, slab, slab],
        out_shape=[jax.ShapeDtypeStruct((n, d), BF16),
                   jax.ShapeDtypeStruct(hk, BF16), jax.ShapeDtypeStruct(hk, BF16),
                   jax.ShapeDtypeStruct(hs, F32), jax.ShapeDtypeStruct(hs, F32)],
        scratch_shapes=[pltpu.VMEM((2 * PEER_HEADS, PEER_KEYS, tile), F32)],
        compiler_params=_params("parallel"),
        name="peer_route",
    )(x, g, wq_t, sk)


def _gelu_sig(x):
    k0 = -2.0 * math.sqrt(2.0 / math.pi) * math.log2(math.e)
    return x / (1.0 + jnp.exp2(x * (k0 + (k0 * 0.044715) * (x * x))))


def _row_bf16(ref, h, i, rows):
    parts = [jnp.broadcast_to(ref[h, c, i:i + 1, :], (16, 128)).astype(BF16)
             for c in range(ref.shape[1])]
    row = jnp.concatenate(parts, axis=1)
    return jnp.broadcast_to(row[None], (rows // 16, 16, row.shape[1])).reshape(rows, row.shape[1])


def _peer_kernel(x_ref, hb_ref, u_ref, vt_ref, r2_ref, e2_ref, nn_ref, p_ref, gf_ref, o_ref,
                 act_ref, w_ref, acc_ref, *, final_norm, act_splits):
    et = pl.program_id(1)
    te = u_ref.shape[0]
    tm = hb_ref.shape[0]
    rows_i = te // PEER_KEYS

    @pl.when(et == 0)
    def _():
        acc_ref[...] = jnp.zeros_like(acc_ref)

    def gating(ii):
        rs = slice(ii * PEER_KEYS, (ii + 1) * PEER_KEYS)
        gate = jnp.zeros((PEER_KEYS, tm), BF16)
        for h in range(PEER_HEADS):
            nn = _row_bf16(nn_ref, h, ii, PEER_KEYS)
            p = _row_bf16(p_ref, h, ii, PEER_KEYS)
            gate = gate + e2_ref[h] * jnp.where(r2_ref[h] < nn, p, jnp.zeros_like(p))
        w_ref[rs, :] = gate * _gelu_sig(act_ref[rs, :]).astype(BF16)

    mrows = te // act_splits
    for m in range(act_splits):
        rs = slice(m * mrows, (m + 1) * mrows)
        act_ref[rs, :] = lax.dot_general(u_ref[rs, :], hb_ref[...], (((1,), (1,)), ((), ())),
                                         preferred_element_type=F32)
        for ii in range(m * rows_i // act_splits, (m + 1) * rows_i // act_splits):
            gating(ii)
    acc_ref[...] += jnp.dot(vt_ref[...], w_ref[...], preferred_element_type=F32)

    @pl.when(et == pl.num_programs(1) - 1)
    def _():
        y = x_ref[...] + acc_ref[...].T
        if final_norm:
            y = _rms(y, gf_ref[...])
        o_ref[...] = y


def _peer_call(x, hb, u, vt, r2, e2, nn, p, g_final, *, final_norm, tm=512, te=1024,
               act_splits=4):
    n, d = x.shape
    rows_i = te // PEER_KEYS
    tok = pl.BlockSpec((tm, d), lambda t, e: (t, 0))
    allj = pl.BlockSpec((PEER_HEADS, PEER_KEYS, tm), lambda t, e: (0, 0, t))
    rowi = pl.BlockSpec((PEER_HEADS, tm // 128, rows_i, 128), lambda t, e: (0, t, e, 0))
    return pl.pallas_call(
        functools.partial(_peer_kernel, final_norm=final_norm, act_splits=act_splits),
        grid=(n // tm, u.shape[0] // te),
        in_specs=[tok, tok,
                  pl.BlockSpec((te, d), lambda t, e: (e, 0)),
                  pl.BlockSpec((d, te), lambda t, e: (0, e)),
                  allj, allj, rowi, rowi,
                  pl.BlockSpec((1, d), lambda t, e: (0, 0))],
        out_specs=tok,
        out_shape=jax.ShapeDtypeStruct((n, d), F32),
        scratch_shapes=[pltpu.VMEM((te, tm), F32), pltpu.VMEM((te, tm), BF16),
                        pltpu.VMEM((d, tm), F32)],
        compiler_params=_params("parallel", "arbitrary"),
        name="peer_dense",
    )(x, hb, u, vt, r2, e2, nn, p, g_final)


def _peer_layer(x, g, w_q, sub_keys, u_tab, v_tab, g_final, final_norm):
    wq_t = w_q.T.astype(BF16)
    sk = sub_keys.reshape(PEER_HEADS * 2, PEER_KEYS, PEER_HALF).astype(BF16)
    hb, r2, e2, nn, p = _route_call(x, g, wq_t, sk)
    return _peer_call(x, hb, u_tab.astype(BF16), v_tab.T.astype(BF16), r2, e2, nn, p, g_final,
                      final_norm=final_norm)


def _rope_tables(seq, batch):
    half = HEAD_DIM // 2
    inv = ROPE_THETA ** (-jnp.arange(half, dtype=F32) / half)
    ang = jnp.arange(seq, dtype=F32)[:, None] * inv[None, :]
    cos = jnp.tile(jnp.cos(ang), (1, 4))
    sin = jnp.sin(ang)
    sin = jnp.concatenate([-sin, -sin, sin, sin], axis=1)
    rep = lambda t: jnp.broadcast_to(t[:, None, :], (seq, batch, 128)).reshape(seq * batch, 128)
    return rep(cos), rep(sin)


def _attention_layer(x, g, w_qkv, w_o, *, batch, seq):
    n, d = x.shape
    n_heads = d // HEAD_DIM
    perm = _rope_perm(n_heads)
    cos, sin = _rope_tables(seq, batch)
    wg = w_qkv.reshape(d, len(DILATED_GROUPS), 3, d)
    outs, lses = [], []
    for gi, (window, dilation) in enumerate(DILATED_GROUPS):
        assert window // dilation == ATTN_BLOCK and (seq // dilation) % ATTN_BLOCK == 0
        w = jnp.concatenate([wg[:, gi, 0][:, perm], wg[:, gi, 1][:, perm], wg[:, gi, 2]],
                            axis=1).astype(BF16)
        q, k, v = _qkv_call(x, g, w, cos, sin, stride=dilation * batch, tile=512)
        o, lse = _attn_call(q, k, v)
        outs.append(o)
        lses.append(lse)
    return _attn_out_call(x, outs, lses, w_o.astype(BF16))


def kernel(x, norm_mix, norm_ffn, norm_final, s5_lam_re, s5_lam_im, s5_log_step, s5_b_re, s5_b_im, s5_c_re, s5_c_im, s5_d, s5_w_glu, attn_w_qkv, attn_w_o, peer_w_q, peer_sub_keys, peer_u, peer_v):
    batch, seq, d = x.shape
    assert batch == 8, "one timestep of all batches must fill one 8-sublane group"
    depth = norm_mix.shape[0]
    xs = x.transpose(1, 0, 2).reshape(seq * batch, d)
    g_final = norm_final.reshape(1, d)
    for i in range(depth):
        j = i // 2
        g_mix = norm_mix[i].reshape(1, d)
        if i % 2 == 0:
            wbu, are, aim, wc = _s5_weights(s5_lam_re[j], s5_lam_im[j], s5_log_step[j],
                                            s5_b_re[j], s5_b_im[j], s5_c_re[j], s5_c_im[j])
            y = _s5_call(xs, g_mix, wbu, are, aim, wc, s5_d[j].reshape(1, d), batch=batch)
            xs = _glu_call(y, xs, s5_w_glu[j].astype(BF16))
        else:
            xs = _attention_layer(xs, g_mix, attn_w_qkv[j], attn_w_o[j], batch=batch, seq=seq)
        xs = _peer_layer(xs, norm_ffn[i].reshape(1, d), peer_w_q[i], peer_sub_keys[i],
                         peer_u[i], peer_v[i], g_final, final_norm=(i == depth - 1))
    return xs.reshape(seq, batch, d).transpose(1, 0, 2)
```

```python
import functools
import math

import jax
import jax.numpy as jnp
import numpy as np
from jax import lax
from jax.experimental import pallas as pl
from jax.experimental.pallas import tpu as pltpu

F32 = jnp.float32
BF16 = jnp.bfloat16

RMS_EPS = 1e-6
SSM_GROUP = 16
SSM_STATE = 64
SSM_BLOCK_GROUPS = 16
HEAD_DIM = 64
DILATED_GROUPS = ((128, 1), (512, 4), (2048, 16))
ATTN_BLOCK = 128
ROPE_THETA = 10000.0
PEER_HEADS = 8
PEER_KEYS = 128
PEER_HALF = 128
PEER_TOPK = 16
NEG_BIG = -1e30

VMEM_LIMIT_BYTES = 56 * 1024 * 1024


def _params(*sem):
    return pltpu.CompilerParams(dimension_semantics=sem, vmem_limit_bytes=VMEM_LIMIT_BYTES)


def _rms(x, g):
    return x * lax.rsqrt(jnp.mean(x * x, axis=-1, keepdims=True) + RMS_EPS) * g


def _gelu(x):
    c = math.sqrt(2.0 / math.pi)
    return 0.5 * x * (1.0 + jnp.tanh(c * (x + 0.044715 * (x * x * x))))


def _s5_kernel(x_ref, g_ref, wbu_ref, are_ref, aim_ref, wc_ref, d_ref, y_ref, bu_ref, st_ref,
               *, batch, nblk):
    @pl.when(pl.program_id(0) == 0)
    def _():
        st_ref[...] = jnp.zeros_like(st_ref)

    rows = x_ref.shape[0]
    steps = rows // batch
    h = _rms(x_ref[...], g_ref[...])
    hb = h.astype(BF16)
    kin = wbu_ref.shape[1]
    half = wbu_ref.shape[2] // 2
    for c in range(nblk):
        bu_ref[...] = jnp.dot(hb[:, c * kin:(c + 1) * kin], wbu_ref[c],
                              preferred_element_type=F32)
        are = jnp.broadcast_to(are_ref[c], (batch, half))
        aim = jnp.broadcast_to(aim_ref[c], (batch, half))

        def step(t, carry):
            sre, sim = carry
            r = pl.multiple_of(t * batch, batch)
            bre = bu_ref[pl.ds(r, batch), 0:half]
            bim = bu_ref[pl.ds(r, batch), half:2 * half]
            nre = are * sre - aim * sim + bre
            nim = are * sim + aim * sre + bim
            bu_ref[pl.ds(r, batch), 0:half] = nre
            bu_ref[pl.ds(r, batch), half:2 * half] = nim
            return nre, nim

        sre, sim = lax.fori_loop(0, steps, step,
                                 (st_ref[c, :, 0:half], st_ref[c, :, half:2 * half]))
        st_ref[c, :, 0:half] = sre
        st_ref[c, :, half:2 * half] = sim
        yc = jnp.dot(bu_ref[...].astype(BF16), wc_ref[c], preferred_element_type=F32)
        yc = yc + d_ref[:, c * kin:(c + 1) * kin] * h[:, c * kin:(c + 1) * kin]
        y_ref[:, c * kin:(c + 1) * kin] = _gelu(yc).astype(BF16)


def _s5_weights(lam_re, lam_im, log_step, b_re, b_im, c_re, c_im):
    G, P = lam_re.shape
    H = b_re.shape[-1]
    step = jnp.exp(log_step)[:, None]
    mag = jnp.exp(lam_re * step)
    lb_re = mag * jnp.cos(lam_im * step)
    lb_im = mag * jnp.sin(lam_im * step)
    den = lam_re * lam_re + lam_im * lam_im
    num_re = lb_re - 1.0
    coef_re = (num_re * lam_re + lb_im * lam_im) / den
    coef_im = (lb_im * lam_re - num_re * lam_im) / den
    bb_re = coef_re[..., None] * b_re - coef_im[..., None] * b_im
    bb_im = coef_re[..., None] * b_im + coef_im[..., None] * b_re
    gb = SSM_BLOCK_GROUPS
    nblk = G // gb
    eye = jnp.eye(gb, dtype=F32)

    def bdiag_in(w):
        w = w.reshape(nblk, gb, P, H)
        return jnp.einsum('cgph,gk->cghkp', w, eye).reshape(nblk, gb * H, gb * P)

    def bdiag_out(w):
        w = w.reshape(nblk, gb, H, P)
        return jnp.einsum('cghp,gk->cgpkh', w, eye).reshape(nblk, gb * P, gb * H)

    wbu = jnp.concatenate([bdiag_in(bb_re), bdiag_in(bb_im)], axis=-1).astype(BF16)
    wc = jnp.concatenate([bdiag_out(c_re), -bdiag_out(c_im)], axis=1).astype(BF16)
    are = lb_re.reshape(nblk, 1, gb * P)
    aim = lb_im.reshape(nblk, 1, gb * P)
    return wbu, are, aim, wc


def _s5_call(x, g, wbu, are, aim, wc, d_skip, *, batch, steps_per_tile=64):
    n, d = x.shape
    nblk = wbu.shape[0]
    rows = batch * steps_per_tile
    assert n % rows == 0
    const3 = lambda i: (0, 0, 0)
    const2 = lambda i: (0, 0)
    return pl.pallas_call(
        functools.partial(_s5_kernel, batch=batch, nblk=nblk),
        grid=(n // rows,),
        in_specs=[pl.BlockSpec((rows, d), lambda i: (i, 0)),
                  pl.BlockSpec((1, d), const2),
                  pl.BlockSpec(wbu.shape, const3),
                  pl.BlockSpec(are.shape, const3),
                  pl.BlockSpec(aim.shape, const3),
                  pl.BlockSpec(wc.shape, const3),
                  pl.BlockSpec((1, d), const2)],
        out_specs=pl.BlockSpec((rows, d), lambda i: (i, 0)),
        out_shape=jax.ShapeDtypeStruct((n, d), BF16),
        scratch_shapes=[pltpu.VMEM((rows, wbu.shape[2]), F32),
                        pltpu.VMEM((nblk, batch, wbu.shape[2]), F32)],
        compiler_params=_params("arbitrary"),
        name="s5_ssm",
    )(x, g, wbu, are, aim, wc, d_skip)


def _glu_kernel(y_ref, x_ref, w_ref, o_ref):
    z = jnp.dot(y_ref[...], w_ref[...], preferred_element_type=F32)
    d = o_ref.shape[1]
    o_ref[...] = x_ref[...] + z[:, :d] * jax.nn.sigmoid(z[:, d:])


def _glu_call(y, x, w, *, tile=512):
    n, d = x.shape
    return pl.pallas_call(
        _glu_kernel,
        grid=(n // tile,),
        in_specs=[pl.BlockSpec((tile, d), lambda i: (i, 0)),
                  pl.BlockSpec((tile, d), lambda i: (i, 0)),
                  pl.BlockSpec(w.shape, lambda i: (0, 0))],
        out_specs=pl.BlockSpec((tile, d), lambda i: (i, 0)),
        out_shape=jax.ShapeDtypeStruct((n, d), F32),
        compiler_params=_params("parallel"),
        name="s5_glu",
    )(y, x, w)


def _rope_perm(n_heads):
    half = HEAD_DIM // 2
    idx = []
    for p in range(n_heads // 2):
        h0, h1 = 2 * p, 2 * p + 1
        for part in (0, 1):
            for h in (h0, h1):
                idx.extend(range(h * HEAD_DIM + part * half, h * HEAD_DIM + (part + 1) * half))
    return np.asarray(idx, dtype=np.int32)


BAND_MACRO = ATTN_BLOCK * 16


def _to_band_order(x, batch, seq):
    d = x.shape[1]
    x = x.reshape(seq // BAND_MACRO, ATTN_BLOCK, 16, batch, d)
    return x.transpose(0, 2, 3, 1, 4).reshape(seq * batch, d)


def _from_band_order(x, batch, seq):
    d = x.shape[1]
    x = x.reshape(seq // BAND_MACRO, 16, batch, ATTN_BLOCK, d)
    return x.transpose(2, 0, 3, 1, 4).reshape(batch, seq, d)


def _band_positions(batch, seq):
    macro = jnp.arange(seq // BAND_MACRO, dtype=jnp.int32)[:, None, None, None]
    r16 = jnp.arange(16, dtype=jnp.int32)[None, :, None, None]
    ml = jnp.arange(ATTN_BLOCK, dtype=jnp.int32)[None, None, None, :]
    t = macro * BAND_MACRO + ml * 16 + r16
    return jnp.broadcast_to(t, (seq // BAND_MACRO, 16, batch, ATTN_BLOCK)).reshape(-1)


def _qkv_kernel(x_ref, g_ref, w_ref, cos_ref, sin_ref, q_ref, k_ref, v_ref):
    d = x_ref.shape[1]
    hb = _rms(x_ref[...], g_ref[...]).astype(BF16)
    cos = cos_ref[...]
    sin = sin_ref[...]
    scale = HEAD_DIM ** -0.5

    def roped(col0, ref, mul):
        for c in range(d // 128):
            blk = jnp.dot(hb, w_ref[:, col0 + c * 128:col0 + (c + 1) * 128],
                          preferred_element_type=F32)
            out = blk * cos + pltpu.roll(blk, 64, axis=1) * sin
            if mul != 1.0:
                out = out * mul
            ref[:, c * 128:(c + 1) * 128] = out.astype(ref.dtype)

    roped(0, q_ref, scale)
    roped(d, k_ref, 1.0)
    v_ref[...] = jnp.dot(hb, w_ref[:, 2 * d:3 * d],
                         preferred_element_type=F32).astype(v_ref.dtype)


def _qkv_call(x, g, w, cos, sin, *, dtype, tile=512):
    n, d = x.shape
    row = pl.BlockSpec((tile, d), lambda i: (i, 0))
    tab = pl.BlockSpec((tile, 128), lambda i: (i, 0))
    out = jax.ShapeDtypeStruct((n, d), dtype)
    return pl.pallas_call(
        _qkv_kernel,
        grid=(n // tile,),
        in_specs=[row, pl.BlockSpec((1, d), lambda i: (0, 0)),
                  pl.BlockSpec(w.shape, lambda i: (0, 0)), tab, tab],
        out_specs=[row, row, row],
        out_shape=[out, out, out],
        compiler_params=_params("parallel"),
        name="attn_qkv",
    )(x, g, w, cos, sin)


def _attn_kernel(q_ref, kp_ref, kc_ref, vp_ref, vc_ref, o_ref, l_ref, *, chunk):
    c = ATTN_BLOCK
    d = q_ref.shape[-1]
    first = pl.program_id(1) == 0

    def pos(r):
        return (r % chunk) * (c // chunk) + r // chunk

    qrow = lax.broadcasted_iota(jnp.int32, (c, 2 * c), 0)
    kcol = lax.broadcasted_iota(jnp.int32, (c, 2 * c), 1)
    cur = kcol >= c
    kpos = jnp.where(cur, pos(kcol - c) + c, pos(kcol))
    dist = pos(qrow) + c - kpos
    valid = (dist >= 0) & (dist <= c) & jnp.logical_or(cur, jnp.logical_not(first))
    lane = lax.broadcasted_iota(jnp.int32, (c, 128), 1)
    qmask0 = (lane // (HEAD_DIM // 2)) % 2 == 0
    omask0 = lane < HEAD_DIM

    def load(ref, sl):
        return ref[..., sl].reshape(c, 128).astype(BF16)

    for p in range(d // 128):
        sl = slice(p * 128, (p + 1) * 128)
        qp = load(q_ref, sl)
        kp = jnp.concatenate([load(kp_ref, sl), load(kc_ref, sl)], axis=0)
        vp = jnp.concatenate([load(vp_ref, sl), load(vc_ref, sl)], axis=0)
        outs, lses = [], []
        for e in range(2):
            qm = jnp.where(qmask0 if e == 0 else jnp.logical_not(qmask0), qp, jnp.zeros_like(qp))
            s = lax.dot_general(qm, kp, (((1,), (1,)), ((), ())), preferred_element_type=F32)
            s = jnp.where(valid, s, NEG_BIG)
            smax = jnp.max(s, axis=-1, keepdims=True)
            ex = jnp.exp(s - smax)
            den = jnp.sum(ex, axis=-1, keepdims=True)
            o = jnp.dot(ex.astype(BF16), vp, preferred_element_type=F32)
            outs.append(o / den)
            lses.append(jnp.broadcast_to(smax + jnp.log(den), (c, 128)))
        blk = o_ref.shape[:-1] + (128,)
        o_ref[..., sl] = jnp.where(omask0, outs[0], outs[1]).astype(o_ref.dtype).reshape(blk)
        l_ref[..., sl] = jnp.where(omask0, lses[0], lses[1]).reshape(blk)


def _attn_call(q, k, v, *, dilation, batch, seq):
    n, d = q.shape
    c = ATTN_BLOCK
    macros = seq // BAND_MACRO
    per = 16 // dilation
    chunk = c // per
    view = (macros, per, dilation * batch, per, chunk, d)
    block = (None, per, None, None, chunk, d)

    def at(s, i):
        return (i // per, 0, s, i % per, 0, 0)

    cur = pl.BlockSpec(block, at)
    prev = pl.BlockSpec(block, lambda s, i: at(s, jnp.maximum(i - 1, 0)))
    o, lse = pl.pallas_call(
        functools.partial(_attn_kernel, chunk=chunk),
        grid=(dilation * batch, macros * per),
        in_specs=[cur, prev, cur, prev, cur],
        out_specs=[cur, cur],
        out_shape=[jax.ShapeDtypeStruct(view, q.dtype), jax.ShapeDtypeStruct(view, F32)],
        compiler_params=_params("parallel", "parallel"),
        name="attn_band",
    )(q.reshape(view), k.reshape(view), k.reshape(view), v.reshape(view), v.reshape(view))
    return o.reshape(n, d), lse.reshape(n, d)


def _attn_out_kernel(x_ref, o0, o1, o2, l0, l1, l2, w_ref, y_ref):
    a, b, c = l0[...], l1[...], l2[...]
    mx = jnp.maximum(jnp.maximum(a, b), c)
    ea, eb, ec = jnp.exp(a - mx), jnp.exp(b - mx), jnp.exp(c - mx)
    inv = 1.0 / (ea + eb + ec)
    o = (ea * inv) * o0[...].astype(F32) + (eb * inv) * o1[...].astype(F32) \
        + (ec * inv) * o2[...].astype(F32)
    y_ref[...] = x_ref[...] + jnp.dot(o.astype(BF16), w_ref[...], preferred_element_type=F32)


def _attn_out_call(x, os_, ls_, w, *, tile=512):
    n, d = x.shape
    row = pl.BlockSpec((tile, d), lambda i: (i, 0))
    return pl.pallas_call(
        _attn_out_kernel,
        grid=(n // tile,),
        in_specs=[row] * 7 + [pl.BlockSpec(w.shape, lambda i: (0, 0))],
        out_specs=row,
        out_shape=jax.ShapeDtypeStruct((n, d), F32),
        compiler_params=_params("parallel"),
        name="attn_out",
    )(x, *os_, *ls_, w)


def _topk_rank(s):
    nk, t = s.shape
    iota = lax.broadcasted_iota(jnp.int32, (nk, t), 0)
    row16 = lax.broadcasted_iota(jnp.int32, (PEER_TOPK, t), 0)
    rank = jnp.full((nk, t), 99, jnp.int32)
    vals = jnp.zeros((PEER_TOPK, t), F32)
    for k in range(PEER_TOPK):
        m = jnp.max(s, axis=0, keepdims=True)
        idx = jnp.min(jnp.where(s == m, iota, nk), axis=0, keepdims=True)
        sel = iota == idx
        rank = jnp.where(sel, k, rank)
        s = jnp.where(sel, -jnp.inf, s)
        vals = jnp.where(row16 == k, m, vals)
    return rank, vals


SENTINEL_BASE = 1e38
SENTINEL_STEP = 1e37
SENTINEL_LIMIT = -0.95e38


def _topk_rank_distinct(s):
    nk, t = s.shape
    row16 = lax.broadcasted_iota(jnp.int32, (PEER_TOPK, t), 0)
    low = jnp.min(s, axis=0, keepdims=True)
    vals = jnp.zeros((PEER_TOPK, t), F32)
    for k in range(PEER_TOPK):
        m = jnp.max(s, axis=0, keepdims=True)
        s = jnp.where(s == m, -(SENTINEL_BASE + k * SENTINEL_STEP), s)
        vals = jnp.where(row16 == k, m, vals)
    top = s < SENTINEL_LIMIT
    rank = jnp.where(top, jnp.floor(s * (-1.0 / SENTINEL_STEP) - (SENTINEL_BASE / SENTINEL_STEP - 0.5)),
                     99.0)
    cnt = jnp.sum(top.astype(F32), axis=0, keepdims=True)
    bad = jnp.logical_or(cnt != float(PEER_TOPK), jnp.logical_not(low > SENTINEL_LIMIT))
    return rank, vals, bad.astype(jnp.int32)


_CAND_BLOCKS = ((0, 0, 8), (0, 8, 8), (1, 0, 8), (2, 0, 5), (3, 0, 4), (4, 0, 3), (5, 0, 2),
                (6, 0, 2), (7, 0, 2))


def _candidates(a, b):
    t = a.shape[1]
    sub = lax.broadcasted_iota(jnp.int32, (8, t), 0)
    cands, flats, valid = [], [], []
    for k, l0, cnt in _CAND_BLOCKS:
        cnd = a[k:k + 1, :] + b[l0:l0 + 8, :]
        if cnt < 8:
            cnd = jnp.where(sub < cnt, cnd, -jnp.inf)
        cands.append(cnd)
        flats.append(sub + (k * PEER_TOPK + l0))
        valid.append(None if cnt == 8 else sub < cnt)
    cands.append(a[8:16, :] + b[0:1, :])
    flats.append((sub + 8) * PEER_TOPK)
    valid.append(None)
    return cands, flats, valid


def _counts_from_sels(sels):
    t = sels[0].shape[1]
    row16 = lax.broadcasted_iota(jnp.int32, (PEER_TOPK, t), 0)
    selfs = [s_.astype(F32) for s_ in sels]
    per_k = [jnp.sum(selfs[0] + selfs[1], axis=0, keepdims=True)]
    per_k += [jnp.sum(s_, axis=0, keepdims=True) for s_ in selfs[2:9]]
    counts = jnp.concatenate([jnp.zeros((8, t), F32), selfs[9]], axis=0)
    for k in range(8):
        counts = jnp.where(row16 == k, per_k[k], counts)
    return counts


def _pair_counts(a, b):
    t = a.shape[1]
    cands, flats, _ = _candidates(a, b)
    top = a[0:1, :] + b[0:1, :]
    sels = [jnp.zeros((8, t), jnp.bool_) for _ in cands]
    zsum = jnp.zeros((1, t), F32)
    big = PEER_TOPK * PEER_TOPK
    for _ in range(PEER_TOPK):
        m = functools.reduce(jnp.maximum, cands)
        m = jnp.max(m, axis=0, keepdims=True)
        idx = functools.reduce(jnp.minimum,
                               [jnp.where(c == m, f, big) for c, f in zip(cands, flats)])
        idx = jnp.min(idx, axis=0, keepdims=True)
        hit = [f == idx for f in flats]
        cands = [jnp.where(h_, -jnp.inf, c) for h_, c in zip(hit, cands)]
        sels = [jnp.logical_or(s_, h_) for s_, h_ in zip(sels, hit)]
        zsum = zsum + jnp.exp(m - top)
    return _counts_from_sels(sels), zsum


def _pair_counts_distinct(a, b):
    cands, _, valid = _candidates(a, b)
    top = a[0:1, :] + b[0:1, :]
    zsum = jnp.zeros((1, a.shape[1]), F32)
    for _ in range(PEER_TOPK):
        m = functools.reduce(jnp.maximum, cands)
        m = jnp.max(m, axis=0, keepdims=True)
        cands = [jnp.where(c == m, -jnp.inf, c) for c in cands]
        zsum = zsum + jnp.exp(m - top)
    sels = [c == -jnp.inf if v is None else jnp.logical_and(c == -jnp.inf, v)
            for c, v in zip(cands, valid)]
    counts = _counts_from_sels(sels)
    total = jnp.sum(counts, axis=0, keepdims=True)
    return counts, zsum, (total != float(PEER_TOPK)).astype(jnp.int32)


def _route_head(s1, s2, exact):
    if exact:
        rank1, a = _topk_rank(s1)
        rank2, b = _topk_rank(s2)
        counts, zsum = _pair_counts(a, b)
        bad = None
    else:
        rank1, a, bad1 = _topk_rank_distinct(s1)
        rank2, b, bad2 = _topk_rank_distinct(s2)
        counts, zsum, bad3 = _pair_counts_distinct(a, b)
        bad = jnp.max(bad1 + bad2 + bad3)
    nn = jnp.zeros(s1.shape, F32)
    for k in range(PEER_TOPK):
        nn = jnp.where(rank1 == k, counts[k:k + 1, :], nn)
    r2 = rank2.astype(F32).astype(BF16)
    e2 = jnp.exp(s2 - b[0:1, :]).astype(BF16)
    p = jnp.exp(s1 - a[0:1, :]) / zsum
    return (r2, e2, nn, p), bad


def _route_kernel(x_ref, g_ref, wq_ref, sk_ref, hb_ref, r2_ref, e2_ref, nn_ref, p_ref, sc_ref):
    hb = _rms(x_ref[...], g_ref[...]).astype(BF16)
    hb_ref[...] = hb
    qt = lax.dot_general(wq_ref[...], hb, (((1,), (1,)), ((), ())),
                         preferred_element_type=F32).astype(BF16)
    for hp in range(2 * PEER_HEADS):
        sc_ref[hp] = jnp.dot(sk_ref[hp], qt[hp * PEER_HALF:(hp + 1) * PEER_HALF],
                             preferred_element_type=F32)
    slabs = nn_ref.shape[1]

    def head(h, carry):
        s1 = sc_ref[2 * h]
        s2 = sc_ref[2 * h + 1]

        def store(vals):
            r2, e2, nn, p = vals
            r2_ref[h] = r2
            e2_ref[h] = e2
            for c in range(slabs):
                nn_ref[h, c] = nn[:, c * 128:(c + 1) * 128]
                p_ref[h, c] = p[:, c * 128:(c + 1) * 128]

        vals, bad = _route_head(s1, s2, exact=False)
        store(vals)

        @pl.when(bad > 0)
        def _():
            store(_route_head(s1, s2, exact=True)[0])

        return carry

    lax.fori_loop(0, PEER_HEADS, head, 0)


def _route_call(x, g, wq_t, sk, *, tile=512):
    n, d = x.shape
    hk = (PEER_HEADS, PEER_KEYS, n)
    hs = (PEER_HEADS, n // 128, PEER_KEYS, 128)
    blk = pl.BlockSpec((PEER_HEADS, PEER_KEYS, tile), lambda i: (0, 0, i))
    slab = pl.BlockSpec((PEER_HEADS, tile // 128, PEER_KEYS, 128), lambda i: (0, i, 0, 0))
    return pl.pallas_call(
        _route_kernel,
        grid=(n // tile,),
        in_specs=[pl.BlockSpec((tile, d), lambda i: (i, 0)),
                  pl.BlockSpec((1, d), lambda i: (0, 0)),
                  pl.BlockSpec(wq_t.shape, lambda i: (0, 0)),
                  pl.BlockSpec(sk.shape, lambda i: (0, 0, 0))],
        out_specs=[pl.BlockSpec((tile, d), lambda i: (i, 0)), blk, blk, slab, slab],
        out_shape=[jax.ShapeDtypeStruct((n, d), BF16),
                   jax.ShapeDtypeStruct(hk, BF16), jax.ShapeDtypeStruct(hk, BF16),
                   jax.ShapeDtypeStruct(hs, F32), jax.ShapeDtypeStruct(hs, F32)],
        scratch_shapes=[pltpu.VMEM((2 * PEER_HEADS, PEER_KEYS, tile), F32)],
        compiler_params=_params("parallel"),
        name="peer_route",
    )(x, g, wq_t, sk)


def _gelu_sig(x):
    k0 = -2.0 * math.sqrt(2.0 / math.pi) * math.log2(math.e)
    return x / (1.0 + jnp.exp2(x * (k0 + (k0 * 0.044715) * (x * x))))


def _row_bf16(ref, h, i, rows):
    parts = [jnp.broadcast_to(ref[h, c, i:i + 1, :], (16, 128)).astype(BF16)
             for c in range(ref.shape[1])]
    row = jnp.concatenate(parts, axis=1)
    return jnp.broadcast_to(row[None], (rows // 16, 16, row.shape[1])).reshape(rows, row.shape[1])


def _peer_kernel(x_ref, hb_ref, u_ref, vt_ref, r2_ref, e2_ref, nn_ref, p_ref, gf_ref, o_ref,
                 act_ref, w_ref, acc_ref, *, final_norm, act_splits):
    et = pl.program_id(1)
    te = u_ref.shape[0]
    tm = hb_ref.shape[0]
    rows_i = te // PEER_KEYS

    @pl.when(et == 0)
    def _():
        acc_ref[...] = jnp.zeros_like(acc_ref)

    def gating(ii):
        rs = slice(ii * PEER_KEYS, (ii + 1) * PEER_KEYS)
        gate = jnp.zeros((PEER_KEYS, tm), BF16)
        for h in range(PEER_HEADS):
            nn = _row_bf16(nn_ref, h, ii, PEER_KEYS)
            p = _row_bf16(p_ref, h, ii, PEER_KEYS)
            gate = gate + e2_ref[h] * jnp.where(r2_ref[h] < nn, p, jnp.zeros_like(p))
        w_ref[rs, :] = gate * _gelu_sig(act_ref[rs, :]).astype(BF16)

    mrows = te // act_splits
    for m in range(act_splits):
        rs = slice(m * mrows, (m + 1) * mrows)
        act_ref[rs, :] = lax.dot_general(u_ref[rs, :], hb_ref[...], (((1,), (1,)), ((), ())),
                                         preferred_element_type=F32)
        for ii in range(m * rows_i // act_splits, (m + 1) * rows_i // act_splits):
            gating(ii)
    acc_ref[...] += jnp.dot(vt_ref[...], w_ref[...], preferred_element_type=F32)

    @pl.when(et == pl.num_programs(1) - 1)
    def _():
        y = x_ref[...] + acc_ref[...].T
        if final_norm:
            y = _rms(y, gf_ref[...])
        o_ref[...] = y


def _peer_call(x, hb, u, vt, r2, e2, nn, p, g_final, *, final_norm, tm=512, te=2048,
               act_splits=4):
    n, d = x.shape
    rows_i = te // PEER_KEYS
    tok = pl.BlockSpec((tm, d), lambda t, e: (t, 0))
    allj = pl.BlockSpec((PEER_HEADS, PEER_KEYS, tm), lambda t, e: (0, 0, t))
    rowi = pl.BlockSpec((PEER_HEADS, tm // 128, rows_i, 128), lambda t, e: (0, t, e, 0))
    return pl.pallas_call(
        functools.partial(_peer_kernel, final_norm=final_norm, act_splits=act_splits),
        grid=(n // tm, u.shape[0] // te),
        in_specs=[tok, tok,
                  pl.BlockSpec((te, d), lambda t, e: (e, 0)),
                  pl.BlockSpec((d, te), lambda t, e: (0, e)),
                  allj, allj, rowi, rowi,
                  pl.BlockSpec((1, d), lambda t, e: (0, 0))],
        out_specs=tok,
        out_shape=jax.ShapeDtypeStruct((n, d), F32),
        scratch_shapes=[pltpu.VMEM((te, tm), F32), pltpu.VMEM((te, tm), BF16),
                        pltpu.VMEM((d, tm), F32)],
        compiler_params=_params("parallel", "arbitrary"),
        name="peer_dense",
    )(x, hb, u, vt, r2, e2, nn, p, g_final)


def _peer_layer(x, g, w_q, sub_keys, u_tab, v_tab, g_final, final_norm):
    wq_t = w_q.T.astype(BF16)
    sk = sub_keys.reshape(PEER_HEADS * 2, PEER_KEYS, PEER_HALF).astype(BF16)
    hb, r2, e2, nn, p = _route_call(x, g, wq_t, sk)
    return _peer_call(x, hb, u_tab.astype(BF16), v_tab.T.astype(BF16), r2, e2, nn, p, g_final,
                      final_norm=final_norm)


def _rope_tables(positions):
    half = HEAD_DIM // 2
    inv = ROPE_THETA ** (-jnp.arange(half, dtype=F32) / half)
    ang = positions.astype(F32)[:, None] * inv[None, :]
    cos = jnp.tile(jnp.cos(ang), (1, 4))
    sin = jnp.sin(ang)
    return cos, jnp.concatenate([-sin, -sin, sin, sin], axis=1)


def _attention_layer(x, g, w_qkv, w_o, *, batch, seq):
    n, d = x.shape
    perm = _rope_perm(d // HEAD_DIM)
    cos, sin = _rope_tables(_band_positions(batch, seq))
    wg = w_qkv.reshape(d, len(DILATED_GROUPS), 3, d)
    outs, lses = [], []
    for gi, (window, dilation) in enumerate(DILATED_GROUPS):
        assert window // dilation == ATTN_BLOCK and 16 % dilation == 0
        w = jnp.concatenate([wg[:, gi, 0][:, perm], wg[:, gi, 1][:, perm], wg[:, gi, 2]],
                            axis=1).astype(BF16)
        dtype = BF16 if (ATTN_BLOCK * dilation // 16) % 16 == 0 else F32
        q, k, v = _qkv_call(x, g, w, cos, sin, dtype=dtype)
        o, lse = _attn_call(q, k, v, dilation=dilation, batch=batch, seq=seq)
        outs.append(o)
        lses.append(lse)
    return _attn_out_call(x, outs, lses, w_o.astype(BF16))


def kernel(x, norm_mix, norm_ffn, norm_final, s5_lam_re, s5_lam_im, s5_log_step, s5_b_re, s5_b_im, s5_c_re, s5_c_im, s5_d, s5_w_glu, attn_w_qkv, attn_w_o, peer_w_q, peer_sub_keys, peer_u, peer_v):
    batch, seq, d = x.shape
    assert batch == 8, "one timestep of all batches must fill one 8-sublane group"
    assert seq % BAND_MACRO == 0
    depth = norm_mix.shape[0]
    xs = x.transpose(1, 0, 2).reshape(seq * batch, d)
    band = False
    g_final = norm_final.reshape(1, d)
    for i in range(depth):
        j = i // 2
        g_mix = norm_mix[i].reshape(1, d)
        if i % 2 == 0:
            if band:
                xs = _from_band_order(xs, batch, seq).transpose(1, 0, 2).reshape(seq * batch, d)
                band = False
            wbu, are, aim, wc = _s5_weights(s5_lam_re[j], s5_lam_im[j], s5_log_step[j],
                                            s5_b_re[j], s5_b_im[j], s5_c_re[j], s5_c_im[j])
            y = _s5_call(xs, g_mix, wbu, are, aim, wc, s5_d[j].reshape(1, d), batch=batch)
            xs = _glu_call(y, xs, s5_w_glu[j].astype(BF16))
        else:
            if not band:
                xs = _to_band_order(xs, batch, seq)
                band = True
            xs = _attention_layer(xs, g_mix, attn_w_qkv[j], attn_w_o[j], batch=batch, seq=seq)
        xs = _peer_layer(xs, norm_ffn[i].reshape(1, d), peer_w_q[i], peer_sub_keys[i],
                         peer_u[i], peer_v[i], g_final, final_norm=(i == depth - 1))
    if band:
        return _from_band_order(xs, batch, seq)
    return xs.reshape(seq, batch, d).transpose(1, 0, 2)
```

```python
import functools
import math

import jax
import jax.numpy as jnp
import numpy as np
from jax import lax
from jax.experimental import pallas as pl
from jax.experimental.pallas import tpu as pltpu

F32 = jnp.float32
BF16 = jnp.bfloat16

RMS_EPS = 1e-6
SSM_GROUP = 16
SSM_STATE = 64
SSM_BLOCK_GROUPS = 16
HEAD_DIM = 64
DILATED_GROUPS = ((128, 1), (512, 4), (2048, 16))
ATTN_BLOCK = 128
ROPE_THETA = 10000.0
PEER_HEADS = 8
PEER_KEYS = 128
PEER_HALF = 128
PEER_TOPK = 16
NEG_BIG = -1e30

VMEM_LIMIT_BYTES = 56 * 1024 * 1024


def _params(*sem):
    return pltpu.CompilerParams(dimension_semantics=sem, vmem_limit_bytes=VMEM_LIMIT_BYTES)


def _rms(x, g):
    return x * lax.rsqrt(jnp.mean(x * x, axis=-1, keepdims=True) + RMS_EPS) * g


def _gelu(x):
    c = math.sqrt(2.0 / math.pi)
    return 0.5 * x * (1.0 + jnp.tanh(c * (x + 0.044715 * (x * x * x))))


def _s5_kernel(x_ref, g_ref, wbu_ref, are_ref, aim_ref, wc_ref, d_ref, y_ref, bu_ref, st_ref,
               *, batch, nblk):
    @pl.when(pl.program_id(0) == 0)
    def _():
        st_ref[...] = jnp.zeros_like(st_ref)

    rows = x_ref.shape[0]
    steps = rows // batch
    h = _rms(x_ref[...], g_ref[...])
    hb = h.astype(BF16)
    kin = wbu_ref.shape[1]
    half = wbu_ref.shape[2] // 2
    for c in range(nblk):
        bu_ref[...] = jnp.dot(hb[:, c * kin:(c + 1) * kin], wbu_ref[c],
                              preferred_element_type=F32)
        are = jnp.broadcast_to(are_ref[c], (batch, half))
        aim = jnp.broadcast_to(aim_ref[c], (batch, half))

        def step(t, carry):
            sre, sim = carry
            r = pl.multiple_of(t * batch, batch)
            bre = bu_ref[pl.ds(r, batch), 0:half]
            bim = bu_ref[pl.ds(r, batch), half:2 * half]
            nre = are * sre - aim * sim + bre
            nim = are * sim + aim * sre + bim
            bu_ref[pl.ds(r, batch), 0:half] = nre
            bu_ref[pl.ds(r, batch), half:2 * half] = nim
            return nre, nim

        sre, sim = lax.fori_loop(0, steps, step,
                                 (st_ref[c, :, 0:half], st_ref[c, :, half:2 * half]))
        st_ref[c, :, 0:half] = sre
        st_ref[c, :, half:2 * half] = sim
        yc = jnp.dot(bu_ref[...].astype(BF16), wc_ref[c], preferred_element_type=F32)
        yc = yc + d_ref[:, c * kin:(c + 1) * kin] * h[:, c * kin:(c + 1) * kin]
        y_ref[:, c * kin:(c + 1) * kin] = _gelu(yc).astype(BF16)


def _s5_weights(lam_re, lam_im, log_step, b_re, b_im, c_re, c_im):
    G, P = lam_re.shape
    H = b_re.shape[-1]
    step = jnp.exp(log_step)[:, None]
    mag = jnp.exp(lam_re * step)
    lb_re = mag * jnp.cos(lam_im * step)
    lb_im = mag * jnp.sin(lam_im * step)
    den = lam_re * lam_re + lam_im * lam_im
    num_re = lb_re - 1.0
    coef_re = (num_re * lam_re + lb_im * lam_im) / den
    coef_im = (lb_im * lam_re - num_re * lam_im) / den
    bb_re = coef_re[..., None] * b_re - coef_im[..., None] * b_im
    bb_im = coef_re[..., None] * b_im + coef_im[..., None] * b_re
    gb = SSM_BLOCK_GROUPS
    nblk = G // gb
    eye = jnp.eye(gb, dtype=F32)

    def bdiag_in(w):
        w = w.reshape(nblk, gb, P, H)
        return jnp.einsum('cgph,gk->cghkp', w, eye).reshape(nblk, gb * H, gb * P)

    def bdiag_out(w):
        w = w.reshape(nblk, gb, H, P)
        return jnp.einsum('cghp,gk->cgpkh', w, eye).reshape(nblk, gb * P, gb * H)

    wbu = jnp.concatenate([bdiag_in(bb_re), bdiag_in(bb_im)], axis=-1).astype(BF16)
    wc = jnp.concatenate([bdiag_out(c_re), -bdiag_out(c_im)], axis=1).astype(BF16)
    are = lb_re.reshape(nblk, 1, gb * P)
    aim = lb_im.reshape(nblk, 1, gb * P)
    return wbu, are, aim, wc


def _s5_call(x, g, wbu, are, aim, wc, d_skip, *, batch, steps_per_tile=64):
    n, d = x.shape
    nblk = wbu.shape[0]
    rows = batch * steps_per_tile
    assert n % rows == 0
    const3 = lambda i: (0, 0, 0)
    const2 = lambda i: (0, 0)
    return pl.pallas_call(
        functools.partial(_s5_kernel, batch=batch, nblk=nblk),
        grid=(n // rows,),
        in_specs=[pl.BlockSpec((rows, d), lambda i: (i, 0)),
                  pl.BlockSpec((1, d), const2),
                  pl.BlockSpec(wbu.shape, const3),
                  pl.BlockSpec(are.shape, const3),
                  pl.BlockSpec(aim.shape, const3),
                  pl.BlockSpec(wc.shape, const3),
                  pl.BlockSpec((1, d), const2)],
        out_specs=pl.BlockSpec((rows, d), lambda i: (i, 0)),
        out_shape=jax.ShapeDtypeStruct((n, d), BF16),
        scratch_shapes=[pltpu.VMEM((rows, wbu.shape[2]), F32),
                        pltpu.VMEM((nblk, batch, wbu.shape[2]), F32)],
        compiler_params=_params("arbitrary"),
        name="s5_ssm",
    )(x, g, wbu, are, aim, wc, d_skip)


def _glu_kernel(y_ref, x_ref, w_ref, o_ref):
    z = jnp.dot(y_ref[...], w_ref[...], preferred_element_type=F32)
    d = o_ref.shape[1]
    o_ref[...] = x_ref[...] + z[:, :d] * jax.nn.sigmoid(z[:, d:])


def _glu_call(y, x, w, *, tile=512):
    n, d = x.shape
    return pl.pallas_call(
        _glu_kernel,
        grid=(n // tile,),
        in_specs=[pl.BlockSpec((tile, d), lambda i: (i, 0)),
                  pl.BlockSpec((tile, d), lambda i: (i, 0)),
                  pl.BlockSpec(w.shape, lambda i: (0, 0))],
        out_specs=pl.BlockSpec((tile, d), lambda i: (i, 0)),
        out_shape=jax.ShapeDtypeStruct((n, d), F32),
        compiler_params=_params("parallel"),
        name="s5_glu",
    )(y, x, w)


def _rope_perm(n_heads):
    half = HEAD_DIM // 2
    idx = []
    for p in range(n_heads // 2):
        h0, h1 = 2 * p, 2 * p + 1
        for part in (0, 1):
            for h in (h0, h1):
                idx.extend(range(h * HEAD_DIM + part * half, h * HEAD_DIM + (part + 1) * half))
    return np.asarray(idx, dtype=np.int32)


BAND_MACRO = ATTN_BLOCK * 16


def _to_band_order(x, batch, seq):
    d = x.shape[1]
    x = x.reshape(seq // BAND_MACRO, ATTN_BLOCK, 16, batch, d)
    return x.transpose(0, 2, 3, 1, 4).reshape(seq * batch, d)


def _from_band_order(x, batch, seq):
    d = x.shape[1]
    x = x.reshape(seq // BAND_MACRO, 16, batch, ATTN_BLOCK, d)
    return x.transpose(2, 0, 3, 1, 4).reshape(batch, seq, d)


def _band_positions(batch, seq):
    macro = jnp.arange(seq // BAND_MACRO, dtype=jnp.int32)[:, None, None, None]
    r16 = jnp.arange(16, dtype=jnp.int32)[None, :, None, None]
    ml = jnp.arange(ATTN_BLOCK, dtype=jnp.int32)[None, None, None, :]
    t = macro * BAND_MACRO + ml * 16 + r16
    return jnp.broadcast_to(t, (seq // BAND_MACRO, 16, batch, ATTN_BLOCK)).reshape(-1)


def _qkv_kernel(x_ref, g_ref, w_ref, cos_ref, sin_ref, q_ref, k_ref, v_ref):
    d = x_ref.shape[1]
    hb = _rms(x_ref[...], g_ref[...]).astype(BF16)
    cos = cos_ref[...]
    sin = sin_ref[...]
    scale = HEAD_DIM ** -0.5

    def roped(col0, ref, mul):
        for c in range(d // 256):
            both = jnp.dot(hb, w_ref[:, col0 + c * 256:col0 + (c + 1) * 256],
                           preferred_element_type=F32)
            for half in range(2):
                blk = both[:, half * 128:(half + 1) * 128]
                out = blk * cos + pltpu.roll(blk, 64, axis=1) * sin
                if mul != 1.0:
                    out = out * mul
                lo = c * 256 + half * 128
                ref[:, lo:lo + 128] = out.astype(ref.dtype)

    roped(0, q_ref, scale)
    roped(d, k_ref, 1.0)
    v_ref[...] = jnp.dot(hb, w_ref[:, 2 * d:3 * d],
                         preferred_element_type=F32).astype(v_ref.dtype)


def _qkv_call(x, g, w, cos, sin, *, dtype, tile=512):
    n, d = x.shape
    row = pl.BlockSpec((tile, d), lambda i: (i, 0))
    tab = pl.BlockSpec((tile, 128), lambda i: (i, 0))
    out = jax.ShapeDtypeStruct((n, d), dtype)
    return pl.pallas_call(
        _qkv_kernel,
        grid=(n // tile,),
        in_specs=[row, pl.BlockSpec((1, d), lambda i: (0, 0)),
                  pl.BlockSpec(w.shape, lambda i: (0, 0)), tab, tab],
        out_specs=[row, row, row],
        out_shape=[out, out, out],
        compiler_params=_params("parallel"),
        name="attn_qkv",
    )(x, g, w, cos, sin)


def _attn_kernel(q_ref, kp_ref, kc_ref, vp_ref, vc_ref, o_ref, l_ref, *, chunk):
    c = ATTN_BLOCK
    d = q_ref.shape[-1]
    first = pl.program_id(1) == 0

    def pos(r):
        return (r % chunk) * (c // chunk) + r // chunk

    qrow = lax.broadcasted_iota(jnp.int32, (c, 2 * c), 0)
    kcol = lax.broadcasted_iota(jnp.int32, (c, 2 * c), 1)
    cur = kcol >= c
    kpos = jnp.where(cur, pos(kcol - c) + c, pos(kcol))
    dist = pos(qrow) + c - kpos
    valid = (dist >= 0) & (dist <= c) & jnp.logical_or(cur, jnp.logical_not(first))
    lane = lax.broadcasted_iota(jnp.int32, (c, 128), 1)
    qmask0 = (lane // (HEAD_DIM // 2)) % 2 == 0
    omask0 = lane < HEAD_DIM

    def load(ref, sl):
        return ref[..., sl].reshape(c, 128).astype(BF16)

    for p in range(d // 128):
        sl = slice(p * 128, (p + 1) * 128)
        qp = load(q_ref, sl)
        kp = jnp.concatenate([load(kp_ref, sl), load(kc_ref, sl)], axis=0)
        vp = jnp.concatenate([load(vp_ref, sl), load(vc_ref, sl)], axis=0)
        outs, lses = [], []
        for e in range(2):
            qm = jnp.where(qmask0 if e == 0 else jnp.logical_not(qmask0), qp, jnp.zeros_like(qp))
            s = lax.dot_general(qm, kp, (((1,), (1,)), ((), ())), preferred_element_type=F32)
            s = jnp.where(valid, s, NEG_BIG)
            smax = jnp.max(s, axis=-1, keepdims=True)
            ex = jnp.exp(s - smax)
            den = jnp.sum(ex, axis=-1, keepdims=True)
            o = jnp.dot(ex.astype(BF16), vp, preferred_element_type=F32)
            outs.append(o / den)
            lses.append(jnp.broadcast_to(smax + jnp.log(den), (c, 128)))
        blk = o_ref.shape[:-1] + (128,)
        o_ref[..., sl] = jnp.where(omask0, outs[0], outs[1]).astype(o_ref.dtype).reshape(blk)
        l_ref[..., sl] = jnp.where(omask0, lses[0], lses[1]).reshape(blk)


def _attn_call(q, k, v, *, dilation, batch, seq):
    n, d = q.shape
    c = ATTN_BLOCK
    macros = seq // BAND_MACRO
    per = 16 // dilation
    chunk = c // per
    view = (macros, per, dilation * batch, per, chunk, d)
    block = (None, per, None, None, chunk, d)

    def at(s, i):
        return (i // per, 0, s, i % per, 0, 0)

    cur = pl.BlockSpec(block, at)
    prev = pl.BlockSpec(block, lambda s, i: at(s, jnp.maximum(i - 1, 0)))
    o, lse = pl.pallas_call(
        functools.partial(_attn_kernel, chunk=chunk),
        grid=(dilation * batch, macros * per),
        in_specs=[cur, prev, cur, prev, cur],
        out_specs=[cur, cur],
        out_shape=[jax.ShapeDtypeStruct(view, q.dtype), jax.ShapeDtypeStruct(view, F32)],
        compiler_params=_params("parallel", "parallel"),
        name="attn_band",
    )(q.reshape(view), k.reshape(view), k.reshape(view), v.reshape(view), v.reshape(view))
    return o.reshape(n, d), lse.reshape(n, d)


def _attn_out_kernel(x_ref, o0, o1, o2, l0, l1, l2, w_ref, y_ref):
    a, b, c = l0[...], l1[...], l2[...]
    mx = jnp.maximum(jnp.maximum(a, b), c)
    ea, eb, ec = jnp.exp(a - mx), jnp.exp(b - mx), jnp.exp(c - mx)
    inv = 1.0 / (ea + eb + ec)
    o = (ea * inv) * o0[...].astype(F32) + (eb * inv) * o1[...].astype(F32) \
        + (ec * inv) * o2[...].astype(F32)
    y_ref[...] = x_ref[...] + jnp.dot(o.astype(BF16), w_ref[...], preferred_element_type=F32)


def _attn_out_call(x, os_, ls_, w, *, tile=512):
    n, d = x.shape
    row = pl.BlockSpec((tile, d), lambda i: (i, 0))
    return pl.pallas_call(
        _attn_out_kernel,
        grid=(n // tile,),
        in_specs=[row] * 7 + [pl.BlockSpec(w.shape, lambda i: (0, 0))],
        out_specs=row,
        out_shape=jax.ShapeDtypeStruct((n, d), F32),
        compiler_params=_params("parallel"),
        name="attn_out",
    )(x, *os_, *ls_, w)


def _topk_rank(s):
    nk, t = s.shape
    iota = lax.broadcasted_iota(jnp.int32, (nk, t), 0)
    row16 = lax.broadcasted_iota(jnp.int32, (PEER_TOPK, t), 0)
    rank = jnp.full((nk, t), 99, jnp.int32)
    vals = jnp.zeros((PEER_TOPK, t), F32)
    for k in range(PEER_TOPK):
        m = jnp.max(s, axis=0, keepdims=True)
        idx = jnp.min(jnp.where(s == m, iota, nk), axis=0, keepdims=True)
        sel = iota == idx
        rank = jnp.where(sel, k, rank)
        s = jnp.where(sel, -jnp.inf, s)
        vals = jnp.where(row16 == k, m, vals)
    return rank, vals


SENTINEL_BASE = 1e38
SENTINEL_STEP = 1e37
SENTINEL_LIMIT = -0.95e38


def _topk_rank_distinct(s):
    nk, t = s.shape
    row16 = lax.broadcasted_iota(jnp.int32, (PEER_TOPK, t), 0)
    low = jnp.min(s, axis=0, keepdims=True)
    vals = jnp.zeros((PEER_TOPK, t), F32)
    for k in range(PEER_TOPK):
        m = jnp.max(s, axis=0, keepdims=True)
        s = jnp.where(s == m, -(SENTINEL_BASE + k * SENTINEL_STEP), s)
        vals = jnp.where(row16 == k, m, vals)
    top = s < SENTINEL_LIMIT
    rank = jnp.where(top, jnp.floor(s * (-1.0 / SENTINEL_STEP) - (SENTINEL_BASE / SENTINEL_STEP - 0.5)),
                     99.0)
    cnt = jnp.sum(top.astype(F32), axis=0, keepdims=True)
    bad = jnp.logical_or(cnt != float(PEER_TOPK), jnp.logical_not(low > SENTINEL_LIMIT))
    return rank, vals, bad.astype(jnp.int32)


_CAND_BLOCKS = ((0, 0, 8), (0, 8, 8), (1, 0, 8), (2, 0, 5), (3, 0, 4), (4, 0, 3), (5, 0, 2),
                (6, 0, 2), (7, 0, 2))


def _candidates(a, b):
    t = a.shape[1]
    sub = lax.broadcasted_iota(jnp.int32, (8, t), 0)
    cands, flats, valid = [], [], []
    for k, l0, cnt in _CAND_BLOCKS:
        cnd = a[k:k + 1, :] + b[l0:l0 + 8, :]
        if cnt < 8:
            cnd = jnp.where(sub < cnt, cnd, -jnp.inf)
        cands.append(cnd)
        flats.append(sub + (k * PEER_TOPK + l0))
        valid.append(None if cnt == 8 else sub < cnt)
    cands.append(a[8:16, :] + b[0:1, :])
    flats.append((sub + 8) * PEER_TOPK)
    valid.append(None)
    return cands, flats, valid


def _counts_from_sels(sels):
    t = sels[0].shape[1]
    row16 = lax.broadcasted_iota(jnp.int32, (PEER_TOPK, t), 0)
    selfs = [s_.astype(F32) for s_ in sels]
    per_k = [jnp.sum(selfs[0] + selfs[1], axis=0, keepdims=True)]
    per_k += [jnp.sum(s_, axis=0, keepdims=True) for s_ in selfs[2:9]]
    counts = jnp.concatenate([jnp.zeros((8, t), F32), selfs[9]], axis=0)
    for k in range(8):
        counts = jnp.where(row16 == k, per_k[k], counts)
    return counts


def _pair_counts(a, b):
    t = a.shape[1]
    cands, flats, _ = _candidates(a, b)
    top = a[0:1, :] + b[0:1, :]
    sels = [jnp.zeros((8, t), jnp.bool_) for _ in cands]
    zsum = jnp.zeros((1, t), F32)
    big = PEER_TOPK * PEER_TOPK
    for _ in range(PEER_TOPK):
        m = functools.reduce(jnp.maximum, cands)
        m = jnp.max(m, axis=0, keepdims=True)
        idx = functools.reduce(jnp.minimum,
                               [jnp.where(c == m, f, big) for c, f in zip(cands, flats)])
        idx = jnp.min(idx, axis=0, keepdims=True)
        hit = [f == idx for f in flats]
        cands = [jnp.where(h_, -jnp.inf, c) for h_, c in zip(hit, cands)]
        sels = [jnp.logical_or(s_, h_) for s_, h_ in zip(sels, hit)]
        zsum = zsum + jnp.exp(m - top)
    return _counts_from_sels(sels), zsum


def _pair_counts_distinct(a, b):
    cands, _, valid = _candidates(a, b)
    top = a[0:1, :] + b[0:1, :]
    zsum = jnp.zeros((1, a.shape[1]), F32)
    for _ in range(PEER_TOPK):
        m = functools.reduce(jnp.maximum, cands)
        m = jnp.max(m, axis=0, keepdims=True)
        cands = [jnp.where(c == m, -jnp.inf, c) for c in cands]
        zsum = zsum + jnp.exp(m - top)
    sels = [c == -jnp.inf if v is None else jnp.logical_and(c == -jnp.inf, v)
            for c, v in zip(cands, valid)]
    counts = _counts_from_sels(sels)
    total = jnp.sum(counts, axis=0, keepdims=True)
    return counts, zsum, (total != float(PEER_TOPK)).astype(jnp.int32)


def _route_head(s1, s2, exact):
    if exact:
        rank1, a = _topk_rank(s1)
        rank2, b = _topk_rank(s2)
        counts, zsum = _pair_counts(a, b)
        bad = None
    else:
        rank1, a, bad1 = _topk_rank_distinct(s1)
        rank2, b, bad2 = _topk_rank_distinct(s2)
        counts, zsum, bad3 = _pair_counts_distinct(a, b)
        bad = jnp.max(bad1 + bad2 + bad3)
    nn = jnp.zeros(s1.shape, F32)
    for k in range(PEER_TOPK):
        nn = jnp.where(rank1 == k, counts[k:k + 1, :], nn)
    r2 = rank2.astype(F32).astype(BF16)
    e2 = jnp.exp(s2 - b[0:1, :]).astype(BF16)
    p = jnp.exp(s1 - a[0:1, :]) / zsum
    return (r2, e2, nn, p), bad


def _route_kernel(x_ref, g_ref, wq_ref, sk_ref, hb_ref, r2_ref, e2_ref, nn_ref, p_ref, sc_ref):
    hb = _rms(x_ref[...], g_ref[...]).astype(BF16)
    hb_ref[...] = hb
    qt = lax.dot_general(wq_ref[...], hb, (((1,), (1,)), ((), ())),
                         preferred_element_type=F32).astype(BF16)
    for hp in range(2 * PEER_HEADS):
        sc_ref[hp] = jnp.dot(sk_ref[hp], qt[hp * PEER_HALF:(hp + 1) * PEER_HALF],
                             preferred_element_type=F32)
    slabs = nn_ref.shape[1]

    def head(h, carry):
        s1 = sc_ref[2 * h]
        s2 = sc_ref[2 * h + 1]

        def store(vals):
            r2, e2, nn, p = vals
            r2_ref[h] = r2
            e2_ref[h] = e2
            for c in range(slabs):
                nn_ref[h, c] = nn[:, c * 128:(c + 1) * 128]
                p_ref[h, c] = p[:, c * 128:(c + 1) * 128]

        vals, bad = _route_head(s1, s2, exact=False)
        store(vals)

        @pl.when(bad > 0)
        def _():
            store(_route_head(s1, s2, exact=True)[0])

        return carry

    lax.fori_loop(0, PEER_HEADS, head, 0)


def _route_call(x, g, wq_t, sk, *, tile=512):
    n, d = x.shape
    hk = (PEER_HEADS, PEER_KEYS, n)
    hs = (PEER_HEADS, n // 128, PEER_KEYS, 128)
    blk = pl.BlockSpec((PEER_HEADS, PEER_KEYS, tile), lambda i: (0, 0, i))
    slab = pl.BlockSpec((PEER_HEADS, tile // 128, PEER_KEYS, 128), lambda i: (0, i, 0, 0))
    return pl.pallas_call(
        _route_kernel,
        grid=(n // tile,),
        in_specs=[pl.BlockSpec((tile, d), lambda i: (i, 0)),
                  pl.BlockSpec((1, d), lambda i: (0, 0)),
                  pl.BlockSpec(wq_t.shape, lambda i: (0, 0)),
                  pl.BlockSpec(sk.shape, lambda i: (0, 0, 0))],
        out_specs=[pl.BlockSpec((tile, d), lambda i: (i, 0)), blk, blk, slab, slab],
        out_shape=[jax.ShapeDtypeStruct((n, d), BF16),
                   jax.ShapeDtypeStruct(hk, BF16), jax.ShapeDtypeStruct(hk, BF16),
                   jax.ShapeDtypeStruct(hs, F32), jax.ShapeDtypeStruct(hs, F32)],
        scratch_shapes=[pltpu.VMEM((2 * PEER_HEADS, PEER_KEYS, tile), F32)],
        compiler_params=_params("parallel"),
        name="peer_route",
    )(x, g, wq_t, sk)


def _gelu_sig(x):
    k0 = -2.0 * math.sqrt(2.0 / math.pi) * math.log2(math.e)
    return x / (1.0 + jnp.exp2(x * (k0 + (k0 * 0.044715) * (x * x))))


def _row_bf16(ref, h, i, rows):
    parts = [jnp.broadcast_to(ref[h, c, i:i + 1, :], (16, 128)).astype(BF16)
             for c in range(ref.shape[1])]
    row = jnp.concatenate(parts, axis=1)
    return jnp.broadcast_to(row[None], (rows // 16, 16, row.shape[1])).reshape(rows, row.shape[1])


def _peer_kernel(x_ref, hb_ref, u_ref, vt_ref, r2_ref, e2_ref, nn_ref, p_ref, gf_ref, o_ref,
                 act_ref, w_ref, acc_ref, *, final_norm, act_splits):
    et = pl.program_id(1)
    te = u_ref.shape[0]
    tm = hb_ref.shape[0]
    rows_i = te // PEER_KEYS

    @pl.when(et == 0)
    def _():
        acc_ref[...] = jnp.zeros_like(acc_ref)

    def gating(ii):
        rs = slice(ii * PEER_KEYS, (ii + 1) * PEER_KEYS)
        gate = jnp.zeros((PEER_KEYS, tm), BF16)
        for h in range(PEER_HEADS):
            nn = _row_bf16(nn_ref, h, ii, PEER_KEYS)
            p = _row_bf16(p_ref, h, ii, PEER_KEYS)
            gate = gate + e2_ref[h] * jnp.where(r2_ref[h] < nn, p, jnp.zeros_like(p))
        w_ref[rs, :] = gate * _gelu_sig(act_ref[rs, :].astype(BF16))

    mrows = te // act_splits
    for m in range(act_splits):
        rs = slice(m * mrows, (m + 1) * mrows)
        act_ref[rs, :] = lax.dot_general(u_ref[rs, :], hb_ref[...], (((1,), (1,)), ((), ())),
                                         preferred_element_type=F32)
        for ii in range(m * rows_i // act_splits, (m + 1) * rows_i // act_splits):
            gating(ii)
    acc_ref[...] += jnp.dot(vt_ref[...], w_ref[...], preferred_element_type=F32)

    @pl.when(et == pl.num_programs(1) - 1)
    def _():
        y = x_ref[...] + acc_ref[...].T
        if final_norm:
            y = _rms(y, gf_ref[...])
        o_ref[...] = y


def _peer_call(x, hb, u, vt, r2, e2, nn, p, g_final, *, final_norm, tm=512, te=2048,
               act_splits=4):
    n, d = x.shape
    rows_i = te // PEER_KEYS
    tok = pl.BlockSpec((tm, d), lambda t, e: (t, 0))
    allj = pl.BlockSpec((PEER_HEADS, PEER_KEYS, tm), lambda t, e: (0, 0, t))
    rowi = pl.BlockSpec((PEER_HEADS, tm // 128, rows_i, 128), lambda t, e: (0, t, e, 0))
    return pl.pallas_call(
        functools.partial(_peer_kernel, final_norm=final_norm, act_splits=act_splits),
        grid=(n // tm, u.shape[0] // te),
        in_specs=[tok, tok,
                  pl.BlockSpec((te, d), lambda t, e: (e, 0)),
                  pl.BlockSpec((d, te), lambda t, e: (0, e)),
                  allj, allj, rowi, rowi,
                  pl.BlockSpec((1, d), lambda t, e: (0, 0))],
        out_specs=tok,
        out_shape=jax.ShapeDtypeStruct((n, d), F32),
        scratch_shapes=[pltpu.VMEM((te, tm), F32), pltpu.VMEM((te, tm), BF16),
                        pltpu.VMEM((d, tm), F32)],
        compiler_params=_params("parallel", "arbitrary"),
        name="peer_dense",
    )(x, hb, u, vt, r2, e2, nn, p, g_final)


def _peer_layer(x, g, w_q, sub_keys, u_tab, v_tab, g_final, final_norm):
    wq_t = w_q.T.astype(BF16)
    sk = sub_keys.reshape(PEER_HEADS * 2, PEER_KEYS, PEER_HALF).astype(BF16)
    hb, r2, e2, nn, p = _route_call(x, g, wq_t, sk)
    return _peer_call(x, hb, u_tab.astype(BF16), v_tab.T.astype(BF16), r2, e2, nn, p, g_final,
                      final_norm=final_norm)


def _rope_tables(positions):
    half = HEAD_DIM // 2
    inv = ROPE_THETA ** (-jnp.arange(half, dtype=F32) / half)
    ang = positions.astype(F32)[:, None] * inv[None, :]
    cos = jnp.tile(jnp.cos(ang), (1, 4))
    sin = jnp.sin(ang)
    return cos, jnp.concatenate([-sin, -sin, sin, sin], axis=1)


def _attention_layer(x, g, w_qkv, w_o, *, batch, seq):
    n, d = x.shape
    perm = _rope_perm(d // HEAD_DIM)
    cos, sin = _rope_tables(_band_positions(batch, seq))
    wg = w_qkv.reshape(d, len(DILATED_GROUPS), 3, d)
    outs, lses = [], []
    for gi, (window, dilation) in enumerate(DILATED_GROUPS):
        assert window // dilation == ATTN_BLOCK and 16 % dilation == 0
        w = jnp.concatenate([wg[:, gi, 0][:, perm], wg[:, gi, 1][:, perm], wg[:, gi, 2]],
                            axis=1).astype(BF16)
        dtype = BF16 if (ATTN_BLOCK * dilation // 16) % 16 == 0 else F32
        q, k, v = _qkv_call(x, g, w, cos, sin, dtype=dtype)
        o, lse = _attn_call(q, k, v, dilation=dilation, batch=batch, seq=seq)
        outs.append(o)
        lses.append(lse)
    return _attn_out_call(x, outs, lses, w_o.astype(BF16))


def kernel(x, norm_mix, norm_ffn, norm_final, s5_lam_re, s5_lam_im, s5_log_step, s5_b_re, s5_b_im, s5_c_re, s5_c_im, s5_d, s5_w_glu, attn_w_qkv, attn_w_o, peer_w_q, peer_sub_keys, peer_u, peer_v):
    batch, seq, d = x.shape
    assert batch == 8, "one timestep of all batches must fill one 8-sublane group"
    assert seq % BAND_MACRO == 0
    depth = norm_mix.shape[0]
    xs = x.transpose(1, 0, 2).reshape(seq * batch, d)
    band = False
    g_final = norm_final.reshape(1, d)
    for i in range(depth):
        j = i // 2
        g_mix = norm_mix[i].reshape(1, d)
        if i % 2 == 0:
            if band:
                xs = _from_band_order(xs, batch, seq).transpose(1, 0, 2).reshape(seq * batch, d)
                band = False
            wbu, are, aim, wc = _s5_weights(s5_lam_re[j], s5_lam_im[j], s5_log_step[j],
                                            s5_b_re[j], s5_b_im[j], s5_c_re[j], s5_c_im[j])
            y = _s5_call(xs, g_mix, wbu, are, aim, wc, s5_d[j].reshape(1, d), batch=batch)
            xs = _glu_call(y, xs, s5_w_glu[j].astype(BF16))
        else:
            if not band:
                xs = _to_band_order(xs, batch, seq)
                band = True
            xs = _attention_layer(xs, g_mix, attn_w_qkv[j], attn_w_o[j], batch=batch, seq=seq)
        xs = _peer_layer(xs, norm_ffn[i].reshape(1, d), peer_w_q[i], peer_sub_keys[i],
                         peer_u[i], peer_v[i], g_final, final_norm=(i == depth - 1))
    if band:
        return _from_band_order(xs, batch, seq)
    return xs.reshape(seq, batch, d).transpose(1, 0, 2)
```

```python
import functools
import math

import jax
import jax.numpy as jnp
import numpy as np
from jax import lax
from jax.experimental import pallas as pl
from jax.experimental.pallas import tpu as pltpu

F32 = jnp.float32
BF16 = jnp.bfloat16

RMS_EPS = 1e-6
SSM_GROUP = 16
SSM_STATE = 64
SSM_BLOCK_GROUPS = 16
HEAD_DIM = 64
DILATED_GROUPS = ((128, 1), (512, 4), (2048, 16))
ATTN_BLOCK = 128
ROPE_THETA = 10000.0
PEER_HEADS = 8
PEER_KEYS = 128
PEER_HALF = 128
PEER_TOPK = 16
NEG_BIG = -1e30

LANES = 128
SUBLANES = 8
BF16_ROWS = 16
MXU_WIDTH = 256
VMEM_LIMIT_BYTES = 56 * 1024 * 1024


def _params(*sem):
    return pltpu.CompilerParams(dimension_semantics=sem, vmem_limit_bytes=VMEM_LIMIT_BYTES)


def _rms(x, g):
    return x * lax.rsqrt(jnp.mean(x * x, axis=-1, keepdims=True) + RMS_EPS) * g


def _gelu(x):
    c = math.sqrt(2.0 / math.pi)
    return 0.5 * x * (1.0 + jnp.tanh(c * (x + 0.044715 * (x * x * x))))


def _s5_kernel(x_ref, g_ref, wbu_ref, are_ref, aim_ref, wc_ref, d_ref, y_ref, bu_ref, st_ref,
               *, batch, nblk):
    @pl.when(pl.program_id(0) == 0)
    def _():
        st_ref[...] = jnp.zeros_like(st_ref)

    rows = x_ref.shape[0]
    steps = rows // batch
    h = _rms(x_ref[...], g_ref[...])
    hb = h.astype(BF16)
    kin = wbu_ref.shape[1]
    half = wbu_ref.shape[2] // 2
    for c in range(nblk):
        bu_ref[...] = jnp.dot(hb[:, c * kin:(c + 1) * kin], wbu_ref[c],
                              preferred_element_type=F32)
        are = jnp.broadcast_to(are_ref[c], (batch, half))
        aim = jnp.broadcast_to(aim_ref[c], (batch, half))

        def step(t, carry):
            sre, sim = carry
            r = pl.multiple_of(t * batch, batch)
            bre = bu_ref[pl.ds(r, batch), 0:half]
            bim = bu_ref[pl.ds(r, batch), half:2 * half]
            nre = are * sre - aim * sim + bre
            nim = are * sim + aim * sre + bim
            bu_ref[pl.ds(r, batch), 0:half] = nre
            bu_ref[pl.ds(r, batch), half:2 * half] = nim
            return nre, nim

        sre, sim = lax.fori_loop(0, steps, step,
                                 (st_ref[c, :, 0:half], st_ref[c, :, half:2 * half]))
        st_ref[c, :, 0:half] = sre
        st_ref[c, :, half:2 * half] = sim
        yc = jnp.dot(bu_ref[...].astype(BF16), wc_ref[c], preferred_element_type=F32)
        yc = yc + d_ref[:, c * kin:(c + 1) * kin] * h[:, c * kin:(c + 1) * kin]
        y_ref[:, c * kin:(c + 1) * kin] = _gelu(yc).astype(BF16)


def _s5_weights(lam_re, lam_im, log_step, b_re, b_im, c_re, c_im):
    G, P = lam_re.shape
    H = b_re.shape[-1]
    step = jnp.exp(log_step)[:, None]
    mag = jnp.exp(lam_re * step)
    lb_re = mag * jnp.cos(lam_im * step)
    lb_im = mag * jnp.sin(lam_im * step)
    den = lam_re * lam_re + lam_im * lam_im
    num_re = lb_re - 1.0
    coef_re = (num_re * lam_re + lb_im * lam_im) / den
    coef_im = (lb_im * lam_re - num_re * lam_im) / den
    bb_re = coef_re[..., None] * b_re - coef_im[..., None] * b_im
    bb_im = coef_re[..., None] * b_im + coef_im[..., None] * b_re
    gb = SSM_BLOCK_GROUPS
    nblk = G // gb
    eye = jnp.eye(gb, dtype=F32)

    def bdiag_in(w):
        w = w.reshape(nblk, gb, P, H)
        return jnp.einsum('cgph,gk->cghkp', w, eye).reshape(nblk, gb * H, gb * P)

    def bdiag_out(w):
        w = w.reshape(nblk, gb, H, P)
        return jnp.einsum('cghp,gk->cgpkh', w, eye).reshape(nblk, gb * P, gb * H)

    wbu = jnp.concatenate([bdiag_in(bb_re), bdiag_in(bb_im)], axis=-1).astype(BF16)
    wc = jnp.concatenate([bdiag_out(c_re), -bdiag_out(c_im)], axis=1).astype(BF16)
    are = lb_re.reshape(nblk, 1, gb * P)
    aim = lb_im.reshape(nblk, 1, gb * P)
    return wbu, are, aim, wc


def _s5_call(x, g, wbu, are, aim, wc, d_skip, *, batch, steps_per_tile=64):
    n, d = x.shape
    nblk = wbu.shape[0]
    rows = batch * steps_per_tile
    assert n % rows == 0
    const3 = lambda i: (0, 0, 0)
    const2 = lambda i: (0, 0)
    return pl.pallas_call(
        functools.partial(_s5_kernel, batch=batch, nblk=nblk),
        grid=(n // rows,),
        in_specs=[pl.BlockSpec((rows, d), lambda i: (i, 0)),
                  pl.BlockSpec((1, d), const2),
                  pl.BlockSpec(wbu.shape, const3),
                  pl.BlockSpec(are.shape, const3),
                  pl.BlockSpec(aim.shape, const3),
                  pl.BlockSpec(wc.shape, const3),
                  pl.BlockSpec((1, d), const2)],
        out_specs=pl.BlockSpec((rows, d), lambda i: (i, 0)),
        out_shape=jax.ShapeDtypeStruct((n, d), BF16),
        scratch_shapes=[pltpu.VMEM((rows, wbu.shape[2]), F32),
                        pltpu.VMEM((nblk, batch, wbu.shape[2]), F32)],
        compiler_params=_params("arbitrary"),
        name="s5_ssm",
    )(x, g, wbu, are, aim, wc, d_skip)


def _glu_kernel(y_ref, x_ref, w_ref, o_ref):
    z = jnp.dot(y_ref[...], w_ref[...], preferred_element_type=F32)
    d = o_ref.shape[1]
    o_ref[...] = x_ref[...] + z[:, :d] * jax.nn.sigmoid(z[:, d:])


def _glu_call(y, x, w, *, tile=512):
    n, d = x.shape
    return pl.pallas_call(
        _glu_kernel,
        grid=(n // tile,),
        in_specs=[pl.BlockSpec((tile, d), lambda i: (i, 0)),
                  pl.BlockSpec((tile, d), lambda i: (i, 0)),
                  pl.BlockSpec(w.shape, lambda i: (0, 0))],
        out_specs=pl.BlockSpec((tile, d), lambda i: (i, 0)),
        out_shape=jax.ShapeDtypeStruct((n, d), F32),
        compiler_params=_params("parallel"),
        name="s5_glu",
    )(y, x, w)


def _rope_perm(n_heads):
    half = HEAD_DIM // 2
    idx = []
    for p in range(n_heads // 2):
        h0, h1 = 2 * p, 2 * p + 1
        for part in (0, 1):
            for h in (h0, h1):
                idx.extend(range(h * HEAD_DIM + part * half, h * HEAD_DIM + (part + 1) * half))
    return np.asarray(idx, dtype=np.int32)


BAND_MACRO = ATTN_BLOCK * 16


def _to_band_order(x, batch, seq):
    d = x.shape[1]
    x = x.reshape(seq // BAND_MACRO, ATTN_BLOCK, 16, batch, d)
    return x.transpose(0, 2, 3, 1, 4).reshape(seq * batch, d)


def _from_band_order(x, batch, seq):
    d = x.shape[1]
    x = x.reshape(seq // BAND_MACRO, 16, batch, ATTN_BLOCK, d)
    return x.transpose(2, 0, 3, 1, 4).reshape(batch, seq, d)


def _band_positions(batch, seq):
    macro = jnp.arange(seq // BAND_MACRO, dtype=jnp.int32)[:, None, None, None]
    r16 = jnp.arange(16, dtype=jnp.int32)[None, :, None, None]
    ml = jnp.arange(ATTN_BLOCK, dtype=jnp.int32)[None, None, None, :]
    t = macro * BAND_MACRO + ml * 16 + r16
    return jnp.broadcast_to(t, (seq // BAND_MACRO, 16, batch, ATTN_BLOCK)).reshape(-1)


def _qkv_kernel(x_ref, g_ref, w_ref, cos_ref, sin_ref, q_ref, k_ref, v_ref):
    d = x_ref.shape[1]
    hb = _rms(x_ref[...], g_ref[...]).astype(BF16)
    cos = cos_ref[...]
    sin = sin_ref[...]
    scale = HEAD_DIM ** -0.5

    def roped(col0, ref, mul):
        for c in range(d // MXU_WIDTH):
            both = jnp.dot(hb, w_ref[:, col0 + c * MXU_WIDTH:col0 + (c + 1) * MXU_WIDTH],
                           preferred_element_type=F32)
            for half in range(MXU_WIDTH // LANES):
                blk = both[:, half * LANES:(half + 1) * LANES]
                out = blk * cos + pltpu.roll(blk, HEAD_DIM, axis=1) * sin
                if mul != 1.0:
                    out = out * mul
                lo = c * MXU_WIDTH + half * LANES
                ref[:, lo:lo + LANES] = out.astype(ref.dtype)

    roped(0, q_ref, scale)
    roped(d, k_ref, 1.0)
    v_ref[...] = jnp.dot(hb, w_ref[:, 2 * d:3 * d],
                         preferred_element_type=F32).astype(v_ref.dtype)


def _qkv_call(x, g, w, cos, sin, *, dtype, tile=512):
    n, d = x.shape
    row = pl.BlockSpec((tile, d), lambda i: (i, 0))
    tab = pl.BlockSpec((tile, LANES), lambda i: (i, 0))
    out = jax.ShapeDtypeStruct((n, d), dtype)
    return pl.pallas_call(
        _qkv_kernel,
        grid=(n // tile,),
        in_specs=[row, pl.BlockSpec((1, d), lambda i: (0, 0)),
                  pl.BlockSpec(w.shape, lambda i: (0, 0)), tab, tab],
        out_specs=[row, row, row],
        out_shape=[out, out, out],
        compiler_params=_params("parallel"),
        name="attn_qkv",
    )(x, g, w, cos, sin)


ATTN_CLASSES_PER_STEP = 4


def _attn_kernel(q_ref, kp_ref, kc_ref, vp_ref, vc_ref, o_ref, l_ref, *, chunk):
    c = ATTN_BLOCK
    d = q_ref.shape[-1]
    first = pl.program_id(1) == 0

    def pos(r):
        return (r % chunk) * (c // chunk) + r // chunk

    qrow = lax.broadcasted_iota(jnp.int32, (c, 2 * c), 0)
    kcol = lax.broadcasted_iota(jnp.int32, (c, 2 * c), 1)
    cur = kcol >= c
    kpos = jnp.where(cur, pos(kcol - c) + c, pos(kcol))
    dist = pos(qrow) + c - kpos
    valid = (dist >= 0) & (dist <= c) & jnp.logical_or(cur, jnp.logical_not(first))
    lane = lax.broadcasted_iota(jnp.int32, (c, LANES), 1)
    qmask0 = (lane // (HEAD_DIM // 2)) % 2 == 0
    omask0 = lane < HEAD_DIM
    blk = (q_ref.shape[0], chunk, LANES)

    def load(ref, cl, sl):
        return ref[:, cl, :, sl].reshape(c, LANES).astype(BF16)

    for cl in range(q_ref.shape[1]):
        for p in range(d // LANES):
            sl = slice(p * LANES, (p + 1) * LANES)
            qp = load(q_ref, cl, sl)
            kp = jnp.concatenate([load(kp_ref, cl, sl), load(kc_ref, cl, sl)], axis=0)
            vp = jnp.concatenate([load(vp_ref, cl, sl), load(vc_ref, cl, sl)], axis=0)
            outs, lses = [], []
            for e in range(2):
                qm = jnp.where(qmask0 if e == 0 else jnp.logical_not(qmask0), qp,
                               jnp.zeros_like(qp))
                s = lax.dot_general(qm, kp, (((1,), (1,)), ((), ())),
                                    preferred_element_type=F32)
                s = jnp.where(valid, s, NEG_BIG)
                smax = jnp.max(s, axis=-1, keepdims=True)
                ex = jnp.exp(s - smax)
                den = jnp.sum(ex, axis=-1, keepdims=True)
                o = jnp.dot(ex.astype(BF16), vp, preferred_element_type=F32)
                outs.append(o / den)
                lses.append(jnp.broadcast_to(smax + jnp.log(den), (c, LANES)))
            o_ref[:, cl, :, sl] = jnp.where(omask0, outs[0], outs[1]).astype(
                o_ref.dtype).reshape(blk)
            l_ref[:, cl, :, sl] = jnp.where(omask0, lses[0], lses[1]).reshape(blk)


def _attn_call(q, k, v, *, dilation, batch, seq):
    n, d = q.shape
    c = ATTN_BLOCK
    macros = seq // BAND_MACRO
    per = 16 // dilation
    chunk = c // per
    ncls = ATTN_CLASSES_PER_STEP
    view = (macros, per, dilation * batch, per, chunk, d)
    block = (None, per, ncls, None, chunk, d)

    def at(s, i):
        return (i // per, 0, s, i % per, 0, 0)

    cur = pl.BlockSpec(block, at)
    prev = pl.BlockSpec(block, lambda s, i: at(s, jnp.maximum(i - 1, 0)))
    o, lse = pl.pallas_call(
        functools.partial(_attn_kernel, chunk=chunk),
        grid=(dilation * batch // ncls, macros * per),
        in_specs=[cur, prev, cur, prev, cur],
        out_specs=[cur, cur],
        out_shape=[jax.ShapeDtypeStruct(view, q.dtype), jax.ShapeDtypeStruct(view, F32)],
        compiler_params=_params("parallel", "parallel"),
        name="attn_band",
    )(q.reshape(view), k.reshape(view), k.reshape(view), v.reshape(view), v.reshape(view))
    return o.reshape(n, d), lse.reshape(n, d)


def _attn_out_kernel(x_ref, o0, o1, o2, l0, l1, l2, w_ref, y_ref):
    a, b, c = l0[...], l1[...], l2[...]
    mx = jnp.maximum(jnp.maximum(a, b), c)
    ea, eb, ec = jnp.exp(a - mx), jnp.exp(b - mx), jnp.exp(c - mx)
    inv = 1.0 / (ea + eb + ec)
    o = (ea * inv) * o0[...].astype(F32) + (eb * inv) * o1[...].astype(F32) \
        + (ec * inv) * o2[...].astype(F32)
    y_ref[...] = x_ref[...] + jnp.dot(o.astype(BF16), w_ref[...], preferred_element_type=F32)


def _attn_out_call(x, os_, ls_, w, *, tile=512):
    n, d = x.shape
    row = pl.BlockSpec((tile, d), lambda i: (i, 0))
    return pl.pallas_call(
        _attn_out_kernel,
        grid=(n // tile,),
        in_specs=[row] * 7 + [pl.BlockSpec(w.shape, lambda i: (0, 0))],
        out_specs=row,
        out_shape=jax.ShapeDtypeStruct((n, d), F32),
        compiler_params=_params("parallel"),
        name="attn_out",
    )(x, *os_, *ls_, w)


def _topk_rank(s):
    nk, t = s.shape
    iota = lax.broadcasted_iota(jnp.int32, (nk, t), 0)
    row16 = lax.broadcasted_iota(jnp.int32, (PEER_TOPK, t), 0)
    rank = jnp.full((nk, t), 99, jnp.int32)
    vals = jnp.zeros((PEER_TOPK, t), F32)
    for k in range(PEER_TOPK):
        m = jnp.max(s, axis=0, keepdims=True)
        idx = jnp.min(jnp.where(s == m, iota, nk), axis=0, keepdims=True)
        sel = iota == idx
        rank = jnp.where(sel, k, rank)
        s = jnp.where(sel, -jnp.inf, s)
        vals = jnp.where(row16 == k, m, vals)
    return rank, vals


SENTINEL_BASE = 1e38
SENTINEL_STEP = 1e37
SENTINEL_LIMIT = -0.95e38


def _topk_rank_distinct(s):
    nk, t = s.shape
    row16 = lax.broadcasted_iota(jnp.int32, (PEER_TOPK, t), 0)
    low = jnp.min(s, axis=0, keepdims=True)
    vals = jnp.zeros((PEER_TOPK, t), F32)
    for k in range(PEER_TOPK):
        m = jnp.max(s, axis=0, keepdims=True)
        s = jnp.where(s == m, -(SENTINEL_BASE + k * SENTINEL_STEP), s)
        vals = jnp.where(row16 == k, m, vals)
    top = s < SENTINEL_LIMIT
    rank = jnp.where(top, jnp.floor(s * (-1.0 / SENTINEL_STEP) - (SENTINEL_BASE / SENTINEL_STEP - 0.5)),
                     99.0)
    cnt = jnp.sum(top.astype(F32), axis=0, keepdims=True)
    bad = jnp.logical_or(cnt != float(PEER_TOPK), jnp.logical_not(low > SENTINEL_LIMIT))
    return rank, vals, bad.astype(jnp.int32)


_CAND_BLOCKS = ((0, 0, 8), (0, 8, 8), (1, 0, 8), (2, 0, 5), (3, 0, 4), (4, 0, 3), (5, 0, 2),
                (6, 0, 2), (7, 0, 2))


def _candidates(a, b):
    t = a.shape[1]
    sub = lax.broadcasted_iota(jnp.int32, (8, t), 0)
    cands, flats, valid = [], [], []
    for k, l0, cnt in _CAND_BLOCKS:
        cnd = a[k:k + 1, :] + b[l0:l0 + 8, :]
        if cnt < 8:
            cnd = jnp.where(sub < cnt, cnd, -jnp.inf)
        cands.append(cnd)
        flats.append(sub + (k * PEER_TOPK + l0))
        valid.append(None if cnt == 8 else sub < cnt)
    cands.append(a[8:16, :] + b[0:1, :])
    flats.append((sub + 8) * PEER_TOPK)
    valid.append(None)
    return cands, flats, valid


def _counts_from_sels(sels):
    t = sels[0].shape[1]
    row16 = lax.broadcasted_iota(jnp.int32, (PEER_TOPK, t), 0)
    selfs = [s_.astype(F32) for s_ in sels]
    per_k = [jnp.sum(selfs[0] + selfs[1], axis=0, keepdims=True)]
    per_k += [jnp.sum(s_, axis=0, keepdims=True) for s_ in selfs[2:9]]
    counts = jnp.concatenate([jnp.zeros((8, t), F32), selfs[9]], axis=0)
    for k in range(8):
        counts = jnp.where(row16 == k, per_k[k], counts)
    return counts


def _pair_counts(a, b):
    t = a.shape[1]
    cands, flats, _ = _candidates(a, b)
    top = a[0:1, :] + b[0:1, :]
    sels = [jnp.zeros((8, t), jnp.bool_) for _ in cands]
    zsum = jnp.zeros((1, t), F32)
    big = PEER_TOPK * PEER_TOPK
    for _ in range(PEER_TOPK):
        m = functools.reduce(jnp.maximum, cands)
        m = jnp.max(m, axis=0, keepdims=True)
        idx = functools.reduce(jnp.minimum,
                               [jnp.where(c == m, f, big) for c, f in zip(cands, flats)])
        idx = jnp.min(idx, axis=0, keepdims=True)
        hit = [f == idx for f in flats]
        cands = [jnp.where(h_, -jnp.inf, c) for h_, c in zip(hit, cands)]
        sels = [jnp.logical_or(s_, h_) for s_, h_ in zip(sels, hit)]
        zsum = zsum + jnp.exp(m - top)
    return _counts_from_sels(sels), zsum


def _pair_counts_distinct(a, b):
    cands, _, valid = _candidates(a, b)
    top = a[0:1, :] + b[0:1, :]
    zsum = jnp.zeros((1, a.shape[1]), F32)
    for _ in range(PEER_TOPK):
        m = functools.reduce(jnp.maximum, cands)
        m = jnp.max(m, axis=0, keepdims=True)
        cands = [jnp.where(c == m, -jnp.inf, c) for c in cands]
        zsum = zsum + jnp.exp(m - top)
    sels = [c == -jnp.inf if v is None else jnp.logical_and(c == -jnp.inf, v)
            for c, v in zip(cands, valid)]
    counts = _counts_from_sels(sels)
    total = jnp.sum(counts, axis=0, keepdims=True)
    return counts, zsum, (total != float(PEER_TOPK)).astype(jnp.int32)


def _route_head(s1, s2, exact):
    if exact:
        rank1, a = _topk_rank(s1)
        rank2, b = _topk_rank(s2)
        counts, zsum = _pair_counts(a, b)
        bad = None
    else:
        rank1, a, bad1 = _topk_rank_distinct(s1)
        rank2, b, bad2 = _topk_rank_distinct(s2)
        counts, zsum, bad3 = _pair_counts_distinct(a, b)
        bad = jnp.max(bad1 + bad2 + bad3)
    idx = rank1.astype(jnp.int32)
    sub = idx & (SUBLANES - 1)
    lo = jnp.take_along_axis(counts[:SUBLANES], sub, axis=0)
    hi = jnp.take_along_axis(counts[SUBLANES:], sub, axis=0)
    nn = jnp.where(idx < SUBLANES, lo, jnp.where(idx < PEER_TOPK, hi, 0.0))
    r2 = rank2.astype(F32).astype(BF16)
    e2 = jnp.exp(s2 - b[0:1, :]).astype(BF16)
    p = jnp.exp(s1 - a[0:1, :]) / zsum
    return (r2, e2, nn, p), bad


def _route_kernel(x_ref, g_ref, wq_ref, sk_ref, hb_ref, r2_ref, e2_ref, nn_ref, p_ref, sc_ref):
    hb = _rms(x_ref[...], g_ref[...]).astype(BF16)
    hb_ref[...] = hb
    qt = lax.dot_general(wq_ref[...], hb, (((1,), (1,)), ((), ())),
                         preferred_element_type=F32).astype(BF16)
    for hp in range(2 * PEER_HEADS):
        sc_ref[hp] = jnp.dot(sk_ref[hp], qt[hp * PEER_HALF:(hp + 1) * PEER_HALF],
                             preferred_element_type=F32)
    slabs = nn_ref.shape[1]

    def head(h, carry):
        s1 = sc_ref[2 * h]
        s2 = sc_ref[2 * h + 1]

        def store(vals):
            r2, e2, nn, p = vals
            r2_ref[h] = r2
            e2_ref[h] = e2
            for c in range(slabs):
                nn_ref[h, c] = nn[:, c * LANES:(c + 1) * LANES]
                p_ref[h, c] = p[:, c * LANES:(c + 1) * LANES]

        vals, bad = _route_head(s1, s2, exact=False)
        store(vals)

        @pl.when(bad > 0)
        def _():
            store(_route_head(s1, s2, exact=True)[0])

        return carry

    lax.fori_loop(0, PEER_HEADS, head, 0)


def _route_call(x, g, wq_t, sk, *, tile=512):
    n, d = x.shape
    hk = (PEER_HEADS, PEER_KEYS, n)
    hs = (PEER_HEADS, n // LANES, PEER_KEYS, LANES)
    blk = pl.BlockSpec((PEER_HEADS, PEER_KEYS, tile), lambda i: (0, 0, i))
    slab = pl.BlockSpec((PEER_HEADS, tile // LANES, PEER_KEYS, LANES), lambda i: (0, i, 0, 0))
    return pl.pallas_call(
        _route_kernel,
        grid=(n // tile,),
        in_specs=[pl.BlockSpec((tile, d), lambda i: (i, 0)),
                  pl.BlockSpec((1, d), lambda i: (0, 0)),
                  pl.BlockSpec(wq_t.shape, lambda i: (0, 0)),
                  pl.BlockSpec(sk.shape, lambda i: (0, 0, 0))],
        out_specs=[pl.BlockSpec((tile, d), lambda i: (i, 0)), blk, blk, slab, slab],
        out_shape=[jax.ShapeDtypeStruct((n, d), BF16),
                   jax.ShapeDtypeStruct(hk, BF16), jax.ShapeDtypeStruct(hk, BF16),
                   jax.ShapeDtypeStruct(hs, F32), jax.ShapeDtypeStruct(hs, F32)],
        scratch_shapes=[pltpu.VMEM((2 * PEER_HEADS, PEER_KEYS, tile), F32)],
        compiler_params=_params("parallel"),
        name="peer_route",
    )(x, g, wq_t, sk)


def _gelu_sig(x):
    k0 = -2.0 * math.sqrt(2.0 / math.pi) * math.log2(math.e)
    return x / (1.0 + jnp.exp2(x * (k0 + (k0 * 0.044715) * (x * x))))


def _row_bf16(ref, h, i, rows):
    parts = [jnp.broadcast_to(ref[h, c, i:i + 1, :], (BF16_ROWS, LANES)).astype(BF16)
             for c in range(ref.shape[1])]
    row = jnp.concatenate(parts, axis=1)
    return jnp.broadcast_to(row[None], (rows // BF16_ROWS, BF16_ROWS, row.shape[1])).reshape(
        rows, row.shape[1])


def _peer_kernel(x_ref, hb_ref, u_ref, vt_ref, r2_ref, e2_ref, nn_ref, p_ref, gf_ref, o_ref,
                 act_ref, w_ref, acc_ref, *, final_norm, act_splits):
    et = pl.program_id(1)
    te = u_ref.shape[0]
    tm = hb_ref.shape[0]
    rows_i = te // PEER_KEYS

    @pl.when(et == 0)
    def _():
        acc_ref[...] = jnp.zeros_like(acc_ref)

    def gating(ii):
        rs = slice(ii * PEER_KEYS, (ii + 1) * PEER_KEYS)
        gate = jnp.zeros((PEER_KEYS, tm), BF16)
        for h in range(PEER_HEADS):
            nn = _row_bf16(nn_ref, h, ii, PEER_KEYS)
            p = _row_bf16(p_ref, h, ii, PEER_KEYS)
            gate = gate + e2_ref[h] * jnp.where(r2_ref[h] < nn, p, jnp.zeros_like(p))
        w_ref[rs, :] = gate * _gelu_sig(act_ref[rs, :].astype(BF16))

    mrows = te // act_splits
    for m in range(act_splits):
        rs = slice(m * mrows, (m + 1) * mrows)
        act_ref[rs, :] = lax.dot_general(u_ref[rs, :], hb_ref[...], (((1,), (1,)), ((), ())),
                                         preferred_element_type=F32)
        for ii in range(m * rows_i // act_splits, (m + 1) * rows_i // act_splits):
            gating(ii)
    acc_ref[...] += jnp.dot(vt_ref[...], w_ref[...], preferred_element_type=F32)

    @pl.when(et == pl.num_programs(1) - 1)
    def _():
        y = x_ref[...] + acc_ref[...].T
        if final_norm:
            y = _rms(y, gf_ref[...])
        o_ref[...] = y


def _peer_call(x, hb, u, vt, r2, e2, nn, p, g_final, *, final_norm, tm=512, te=2048,
               act_splits=4):
    n, d = x.shape
    rows_i = te // PEER_KEYS
    tok = pl.BlockSpec((tm, d), lambda t, e: (t, 0))
    allj = pl.BlockSpec((PEER_HEADS, PEER_KEYS, tm), lambda t, e: (0, 0, t))
    rowi = pl.BlockSpec((PEER_HEADS, tm // LANES, rows_i, LANES), lambda t, e: (0, t, e, 0))
    return pl.pallas_call(
        functools.partial(_peer_kernel, final_norm=final_norm, act_splits=act_splits),
        grid=(n // tm, u.shape[0] // te),
        in_specs=[tok, tok,
                  pl.BlockSpec((te, d), lambda t, e: (e, 0)),
                  pl.BlockSpec((d, te), lambda t, e: (0, e)),
                  allj, allj, rowi, rowi,
                  pl.BlockSpec((1, d), lambda t, e: (0, 0))],
        out_specs=tok,
        out_shape=jax.ShapeDtypeStruct((n, d), F32),
        scratch_shapes=[pltpu.VMEM((te, tm), F32), pltpu.VMEM((te, tm), BF16),
                        pltpu.VMEM((d, tm), F32)],
        compiler_params=_params("parallel", "arbitrary"),
        name="peer_dense",
    )(x, hb, u, vt, r2, e2, nn, p, g_final)


def _peer_layer(x, g, w_q, sub_keys, u_tab, v_tab, g_final, final_norm):
    wq_t = w_q.T.astype(BF16)
    sk = sub_keys.reshape(PEER_HEADS * 2, PEER_KEYS, PEER_HALF).astype(BF16)
    hb, r2, e2, nn, p = _route_call(x, g, wq_t, sk)
    return _peer_call(x, hb, u_tab.astype(BF16), v_tab.T.astype(BF16), r2, e2, nn, p, g_final,
                      final_norm=final_norm)


def _rope_tables(positions):
    half = HEAD_DIM // 2
    inv = ROPE_THETA ** (-jnp.arange(half, dtype=F32) / half)
    ang = positions.astype(F32)[:, None] * inv[None, :]
    cos = jnp.tile(jnp.cos(ang), (1, 4))
    sin = jnp.sin(ang)
    return cos, jnp.concatenate([-sin, -sin, sin, sin], axis=1)


def _attention_layer(x, g, w_qkv, w_o, *, batch, seq):
    n, d = x.shape
    perm = _rope_perm(d // HEAD_DIM)
    cos, sin = _rope_tables(_band_positions(batch, seq))
    wg = w_qkv.reshape(d, len(DILATED_GROUPS), 3, d)
    outs, lses = [], []
    for gi, (window, dilation) in enumerate(DILATED_GROUPS):
        assert window // dilation == ATTN_BLOCK and 16 % dilation == 0
        w = jnp.concatenate([wg[:, gi, 0][:, perm], wg[:, gi, 1][:, perm], wg[:, gi, 2]],
                            axis=1).astype(BF16)
        dtype = BF16 if (ATTN_BLOCK * dilation // 16) % 16 == 0 else F32
        q, k, v = _qkv_call(x, g, w, cos, sin, dtype=dtype)
        o, lse = _attn_call(q, k, v, dilation=dilation, batch=batch, seq=seq)
        outs.append(o)
        lses.append(lse)
    return _attn_out_call(x, outs, lses, w_o.astype(BF16))


def kernel(x, norm_mix, norm_ffn, norm_final, s5_lam_re, s5_lam_im, s5_log_step, s5_b_re, s5_b_im, s5_c_re, s5_c_im, s5_d, s5_w_glu, attn_w_qkv, attn_w_o, peer_w_q, peer_sub_keys, peer_u, peer_v):
    batch, seq, d = x.shape
    assert batch == 8, "one timestep of all batches must fill one 8-sublane group"
    assert seq % BAND_MACRO == 0
    depth = norm_mix.shape[0]
    xs = x.transpose(1, 0, 2).reshape(seq * batch, d)
    band = False
    g_final = norm_final.reshape(1, d)
    for i in range(depth):
        j = i // 2
        g_mix = norm_mix[i].reshape(1, d)
        if i % 2 == 0:
            if band:
                xs = _from_band_order(xs, batch, seq).transpose(1, 0, 2).reshape(seq * batch, d)
                band = False
            wbu, are, aim, wc = _s5_weights(s5_lam_re[j], s5_lam_im[j], s5_log_step[j],
                                            s5_b_re[j], s5_b_im[j], s5_c_re[j], s5_c_im[j])
            y = _s5_call(xs, g_mix, wbu, are, aim, wc, s5_d[j].reshape(1, d), batch=batch)
            xs = _glu_call(y, xs, s5_w_glu[j].astype(BF16))
        else:
            if not band:
                xs = _to_band_order(xs, batch, seq)
                band = True
            xs = _attention_layer(xs, g_mix, attn_w_qkv[j], attn_w_o[j], batch=batch, seq=seq)
        xs = _peer_layer(xs, norm_ffn[i].reshape(1, d), peer_w_q[i], peer_sub_keys[i],
                         peer_u[i], peer_v[i], g_final, final_norm=(i == depth - 1))
    if band:
        return _from_band_order(xs, batch, seq)
    return xs.reshape(seq, batch, d).transpose(1, 0, 2)
```

```python
import functools
import math

import jax
import jax.numpy as jnp
import numpy as np
from jax import lax
from jax.experimental import pallas as pl
from jax.experimental.pallas import tpu as pltpu

F32 = jnp.float32
BF16 = jnp.bfloat16

RMS_EPS = 1e-6
SSM_GROUP = 16
SSM_STATE = 64
SSM_BLOCK_GROUPS = 16
HEAD_DIM = 64
DILATED_GROUPS = ((128, 1), (512, 4), (2048, 16))
ATTN_BLOCK = 128
ROPE_THETA = 10000.0
PEER_HEADS = 8
PEER_KEYS = 128
PEER_HALF = 128
PEER_TOPK = 16
NEG_BIG = -1e30

LANES = 128
SUBLANES = 8
BF16_ROWS = 16
MXU_WIDTH = 256
VMEM_LIMIT_BYTES = 56 * 1024 * 1024


def _params(*sem):
    return pltpu.CompilerParams(dimension_semantics=sem, vmem_limit_bytes=VMEM_LIMIT_BYTES)


def _rms(x, g):
    return x * lax.rsqrt(jnp.mean(x * x, axis=-1, keepdims=True) + RMS_EPS) * g


def _gelu(x):
    c = math.sqrt(2.0 / math.pi)
    return 0.5 * x * (1.0 + jnp.tanh(c * (x + 0.044715 * (x * x * x))))


def _s5_kernel(x_ref, g_ref, wbu_ref, are_ref, aim_ref, wc_ref, d_ref, y_ref, bu_ref, st_ref,
               *, batch, nblk):
    @pl.when(pl.program_id(0) == 0)
    def _():
        st_ref[...] = jnp.zeros_like(st_ref)

    rows = x_ref.shape[0]
    steps = rows // batch
    h = _rms(x_ref[...], g_ref[...])
    hb = h.astype(BF16)
    kin = wbu_ref.shape[1]
    half = wbu_ref.shape[2] // 2
    for c in range(nblk):
        bu_ref[...] = jnp.dot(hb[:, c * kin:(c + 1) * kin], wbu_ref[c],
                              preferred_element_type=F32)
        are = jnp.broadcast_to(are_ref[c], (batch, half))
        aim = jnp.broadcast_to(aim_ref[c], (batch, half))

        def step(t, carry):
            sre, sim = carry
            r = pl.multiple_of(t * batch, batch)
            bre = bu_ref[pl.ds(r, batch), 0:half]
            bim = bu_ref[pl.ds(r, batch), half:2 * half]
            nre = are * sre - aim * sim + bre
            nim = are * sim + aim * sre + bim
            bu_ref[pl.ds(r, batch), 0:half] = nre
            bu_ref[pl.ds(r, batch), half:2 * half] = nim
            return nre, nim

        sre, sim = lax.fori_loop(0, steps, step,
                                 (st_ref[c, :, 0:half], st_ref[c, :, half:2 * half]),
                                 unroll=True)
        st_ref[c, :, 0:half] = sre
        st_ref[c, :, half:2 * half] = sim
        yc = jnp.dot(bu_ref[...].astype(BF16), wc_ref[c], preferred_element_type=F32)
        yc = yc + d_ref[:, c * kin:(c + 1) * kin] * h[:, c * kin:(c + 1) * kin]
        y_ref[:, c * kin:(c + 1) * kin] = _gelu(yc).astype(BF16)


def _s5_weights(lam_re, lam_im, log_step, b_re, b_im, c_re, c_im):
    G, P = lam_re.shape
    H = b_re.shape[-1]
    step = jnp.exp(log_step)[:, None]
    mag = jnp.exp(lam_re * step)
    lb_re = mag * jnp.cos(lam_im * step)
    lb_im = mag * jnp.sin(lam_im * step)
    den = lam_re * lam_re + lam_im * lam_im
    num_re = lb_re - 1.0
    coef_re = (num_re * lam_re + lb_im * lam_im) / den
    coef_im = (lb_im * lam_re - num_re * lam_im) / den
    bb_re = coef_re[..., None] * b_re - coef_im[..., None] * b_im
    bb_im = coef_re[..., None] * b_im + coef_im[..., None] * b_re
    gb = SSM_BLOCK_GROUPS
    nblk = G // gb
    eye = jnp.eye(gb, dtype=F32)

    def bdiag_in(w):
        w = w.reshape(nblk, gb, P, H)
        return jnp.einsum('cgph,gk->cghkp', w, eye).reshape(nblk, gb * H, gb * P)

    def bdiag_out(w):
        w = w.reshape(nblk, gb, H, P)
        return jnp.einsum('cghp,gk->cgpkh', w, eye).reshape(nblk, gb * P, gb * H)

    wbu = jnp.concatenate([bdiag_in(bb_re), bdiag_in(bb_im)], axis=-1).astype(BF16)
    wc = jnp.concatenate([bdiag_out(c_re), -bdiag_out(c_im)], axis=1).astype(BF16)
    are = lb_re.reshape(nblk, 1, gb * P)
    aim = lb_im.reshape(nblk, 1, gb * P)
    return wbu, are, aim, wc


def _s5_call(x, g, wbu, are, aim, wc, d_skip, *, batch, steps_per_tile=64):
    n, d = x.shape
    nblk = wbu.shape[0]
    rows = batch * steps_per_tile
    assert n % rows == 0
    const3 = lambda i: (0, 0, 0)
    const2 = lambda i: (0, 0)
    return pl.pallas_call(
        functools.partial(_s5_kernel, batch=batch, nblk=nblk),
        grid=(n // rows,),
        in_specs=[pl.BlockSpec((rows, d), lambda i: (i, 0)),
                  pl.BlockSpec((1, d), const2),
                  pl.BlockSpec(wbu.shape, const3),
                  pl.BlockSpec(are.shape, const3),
                  pl.BlockSpec(aim.shape, const3),
                  pl.BlockSpec(wc.shape, const3),
                  pl.BlockSpec((1, d), const2)],
        out_specs=pl.BlockSpec((rows, d), lambda i: (i, 0)),
        out_shape=jax.ShapeDtypeStruct((n, d), BF16),
        scratch_shapes=[pltpu.VMEM((rows, wbu.shape[2]), F32),
                        pltpu.VMEM((nblk, batch, wbu.shape[2]), F32)],
        compiler_params=_params("arbitrary"),
        name="s5_ssm",
    )(x, g, wbu, are, aim, wc, d_skip)


def _glu_kernel(y_ref, x_ref, w_ref, o_ref):
    z = jnp.dot(y_ref[...], w_ref[...], preferred_element_type=F32)
    d = o_ref.shape[1]
    o_ref[...] = x_ref[...] + z[:, :d] * jax.nn.sigmoid(z[:, d:])


def _glu_call(y, x, w, *, tile=512):
    n, d = x.shape
    return pl.pallas_call(
        _glu_kernel,
        grid=(n // tile,),
        in_specs=[pl.BlockSpec((tile, d), lambda i: (i, 0)),
                  pl.BlockSpec((tile, d), lambda i: (i, 0)),
                  pl.BlockSpec(w.shape, lambda i: (0, 0))],
        out_specs=pl.BlockSpec((tile, d), lambda i: (i, 0)),
        out_shape=jax.ShapeDtypeStruct((n, d), F32),
        compiler_params=_params("parallel"),
        name="s5_glu",
    )(y, x, w)


def _rope_perm(n_heads):
    half = HEAD_DIM // 2
    idx = []
    for p in range(n_heads // 2):
        h0, h1 = 2 * p, 2 * p + 1
        for part in (0, 1):
            for h in (h0, h1):
                idx.extend(range(h * HEAD_DIM + part * half, h * HEAD_DIM + (part + 1) * half))
    return np.asarray(idx, dtype=np.int32)


BAND_MACRO = ATTN_BLOCK * 16


def _to_band_order(x, batch, seq):
    d = x.shape[1]
    x = x.reshape(seq // BAND_MACRO, ATTN_BLOCK, 16, batch, d)
    return x.transpose(0, 2, 3, 1, 4).reshape(seq * batch, d)


def _from_band_order(x, batch, seq):
    d = x.shape[1]
    x = x.reshape(seq // BAND_MACRO, 16, batch, ATTN_BLOCK, d)
    return x.transpose(2, 0, 3, 1, 4).reshape(batch, seq, d)


def _band_positions(batch, seq):
    macro = jnp.arange(seq // BAND_MACRO, dtype=jnp.int32)[:, None, None, None]
    r16 = jnp.arange(16, dtype=jnp.int32)[None, :, None, None]
    ml = jnp.arange(ATTN_BLOCK, dtype=jnp.int32)[None, None, None, :]
    t = macro * BAND_MACRO + ml * 16 + r16
    return jnp.broadcast_to(t, (seq // BAND_MACRO, 16, batch, ATTN_BLOCK)).reshape(-1)


def _qkv_kernel(x_ref, g_ref, w_ref, cos_ref, sin_ref, q_ref, k_ref, v_ref):
    d = x_ref.shape[1]
    hb = _rms(x_ref[...], g_ref[...]).astype(BF16)
    cos = cos_ref[...]
    sin = sin_ref[...]
    scale = HEAD_DIM ** -0.5

    def roped(col0, ref, mul):
        for c in range(d // MXU_WIDTH):
            both = jnp.dot(hb, w_ref[:, col0 + c * MXU_WIDTH:col0 + (c + 1) * MXU_WIDTH],
                           preferred_element_type=F32)
            for half in range(MXU_WIDTH // LANES):
                blk = both[:, half * LANES:(half + 1) * LANES]
                out = blk * cos + pltpu.roll(blk, HEAD_DIM, axis=1) * sin
                if mul != 1.0:
                    out = out * mul
                lo = c * MXU_WIDTH + half * LANES
                ref[:, lo:lo + LANES] = out.astype(ref.dtype)

    roped(0, q_ref, scale)
    roped(d, k_ref, 1.0)
    v_ref[...] = jnp.dot(hb, w_ref[:, 2 * d:3 * d],
                         preferred_element_type=F32).astype(v_ref.dtype)


def _qkv_call(x, g, w, cos, sin, *, dtype, tile=512):
    n, d = x.shape
    row = pl.BlockSpec((tile, d), lambda i: (i, 0))
    tab = pl.BlockSpec((tile, LANES), lambda i: (i, 0))
    out = jax.ShapeDtypeStruct((n, d), dtype)
    return pl.pallas_call(
        _qkv_kernel,
        grid=(n // tile,),
        in_specs=[row, pl.BlockSpec((1, d), lambda i: (0, 0)),
                  pl.BlockSpec(w.shape, lambda i: (0, 0)), tab, tab],
        out_specs=[row, row, row],
        out_shape=[out, out, out],
        compiler_params=_params("parallel"),
        name="attn_qkv",
    )(x, g, w, cos, sin)


ATTN_CLASSES_PER_STEP = 4


def _attn_kernel(q_ref, kp_ref, kc_ref, vp_ref, vc_ref, o_ref, l_ref, *, chunk):
    c = ATTN_BLOCK
    d = q_ref.shape[-1]
    first = pl.program_id(1) == 0

    def pos(r):
        return (r % chunk) * (c // chunk) + r // chunk

    qrow = lax.broadcasted_iota(jnp.int32, (c, 2 * c), 0)
    kcol = lax.broadcasted_iota(jnp.int32, (c, 2 * c), 1)
    cur = kcol >= c
    kpos = jnp.where(cur, pos(kcol - c) + c, pos(kcol))
    dist = pos(qrow) + c - kpos
    valid = (dist >= 0) & (dist <= c) & jnp.logical_or(cur, jnp.logical_not(first))
    lane = lax.broadcasted_iota(jnp.int32, (c, LANES), 1)
    qmask0 = (lane // (HEAD_DIM // 2)) % 2 == 0
    omask0 = lane < HEAD_DIM
    blk = (q_ref.shape[0], chunk, LANES)

    def load(ref, cl, sl):
        return ref[:, cl, :, sl].reshape(c, LANES).astype(BF16)

    for cl in range(q_ref.shape[1]):
        for p in range(d // LANES):
            sl = slice(p * LANES, (p + 1) * LANES)
            qp = load(q_ref, cl, sl)
            kp = jnp.concatenate([load(kp_ref, cl, sl), load(kc_ref, cl, sl)], axis=0)
            vp = jnp.concatenate([load(vp_ref, cl, sl), load(vc_ref, cl, sl)], axis=0)
            outs, lses = [], []
            for e in range(2):
                qm = jnp.where(qmask0 if e == 0 else jnp.logical_not(qmask0), qp,
                               jnp.zeros_like(qp))
                s = lax.dot_general(qm, kp, (((1,), (1,)), ((), ())),
                                    preferred_element_type=F32)
                s = jnp.where(valid, s, NEG_BIG)
                smax = jnp.max(s, axis=-1, keepdims=True)
                ex = jnp.exp(s - smax)
                den = jnp.sum(ex, axis=-1, keepdims=True)
                o = jnp.dot(ex.astype(BF16), vp, preferred_element_type=F32)
                outs.append(o / den)
                lses.append(jnp.broadcast_to(smax + jnp.log(den), (c, LANES)))
            o_ref[:, cl, :, sl] = jnp.where(omask0, outs[0], outs[1]).astype(
                o_ref.dtype).reshape(blk)
            l_ref[:, cl, :, sl] = jnp.where(omask0, lses[0], lses[1]).reshape(blk)


def _attn_call(q, k, v, *, dilation, batch, seq):
    n, d = q.shape
    c = ATTN_BLOCK
    macros = seq // BAND_MACRO
    per = 16 // dilation
    chunk = c // per
    ncls = ATTN_CLASSES_PER_STEP
    view = (macros, per, dilation * batch, per, chunk, d)
    block = (None, per, ncls, None, chunk, d)

    def at(s, i):
        return (i // per, 0, s, i % per, 0, 0)

    cur = pl.BlockSpec(block, at)
    prev = pl.BlockSpec(block, lambda s, i: at(s, jnp.maximum(i - 1, 0)))
    o, lse = pl.pallas_call(
        functools.partial(_attn_kernel, chunk=chunk),
        grid=(dilation * batch // ncls, macros * per),
        in_specs=[cur, prev, cur, prev, cur],
        out_specs=[cur, cur],
        out_shape=[jax.ShapeDtypeStruct(view, q.dtype), jax.ShapeDtypeStruct(view, F32)],
        compiler_params=_params("parallel", "parallel"),
        name="attn_band",
    )(q.reshape(view), k.reshape(view), k.reshape(view), v.reshape(view), v.reshape(view))
    return o.reshape(n, d), lse.reshape(n, d)


def _attn_out_kernel(x_ref, o0, o1, o2, l0, l1, l2, w_ref, y_ref):
    a, b, c = l0[...], l1[...], l2[...]
    mx = jnp.maximum(jnp.maximum(a, b), c)
    ea, eb, ec = jnp.exp(a - mx), jnp.exp(b - mx), jnp.exp(c - mx)
    inv = 1.0 / (ea + eb + ec)
    o = (ea * inv) * o0[...].astype(F32) + (eb * inv) * o1[...].astype(F32) \
        + (ec * inv) * o2[...].astype(F32)
    y_ref[...] = x_ref[...] + jnp.dot(o.astype(BF16), w_ref[...], preferred_element_type=F32)


def _attn_out_call(x, os_, ls_, w, *, tile=512):
    n, d = x.shape
    row = pl.BlockSpec((tile, d), lambda i: (i, 0))
    return pl.pallas_call(
        _attn_out_kernel,
        grid=(n // tile,),
        in_specs=[row] * 7 + [pl.BlockSpec(w.shape, lambda i: (0, 0))],
        out_specs=row,
        out_shape=jax.ShapeDtypeStruct((n, d), F32),
        compiler_params=_params("parallel"),
        name="attn_out",
    )(x, *os_, *ls_, w)


def _topk_rank(s):
    nk, t = s.shape
    iota = lax.broadcasted_iota(jnp.int32, (nk, t), 0)
    row16 = lax.broadcasted_iota(jnp.int32, (PEER_TOPK, t), 0)
    rank = jnp.full((nk, t), 99, jnp.int32)
    vals = jnp.zeros((PEER_TOPK, t), F32)
    for k in range(PEER_TOPK):
        m = jnp.max(s, axis=0, keepdims=True)
        idx = jnp.min(jnp.where(s == m, iota, nk), axis=0, keepdims=True)
        sel = iota == idx
        rank = jnp.where(sel, k, rank)
        s = jnp.where(sel, -jnp.inf, s)
        vals = jnp.where(row16 == k, m, vals)
    return rank, vals


SENTINEL_BASE = 1e38
SENTINEL_STEP = 1e37
SENTINEL_LIMIT = -0.95e38


def _topk_rank_distinct(s):
    nk, t = s.shape
    row16 = lax.broadcasted_iota(jnp.int32, (PEER_TOPK, t), 0)
    low = jnp.min(s, axis=0, keepdims=True)
    vals = jnp.zeros((PEER_TOPK, t), F32)
    for k in range(PEER_TOPK):
        m = jnp.max(s, axis=0, keepdims=True)
        s = jnp.where(s == m, -(SENTINEL_BASE + k * SENTINEL_STEP), s)
        vals = jnp.where(row16 == k, m, vals)
    top = s < SENTINEL_LIMIT
    rank = jnp.where(top, jnp.floor(s * (-1.0 / SENTINEL_STEP) - (SENTINEL_BASE / SENTINEL_STEP - 0.5)),
                     99.0)
    cnt = jnp.sum(top.astype(F32), axis=0, keepdims=True)
    bad = jnp.logical_or(cnt != float(PEER_TOPK), jnp.logical_not(low > SENTINEL_LIMIT))
    return rank, vals, bad.astype(jnp.int32)


_CAND_BLOCKS = ((0, 0, 8), (0, 8, 8), (1, 0, 8), (2, 0, 5), (3, 0, 4), (4, 0, 3), (5, 0, 2),
                (6, 0, 2), (7, 0, 2))


def _candidates(a, b):
    t = a.shape[1]
    sub = lax.broadcasted_iota(jnp.int32, (8, t), 0)
    cands, flats, valid = [], [], []
    for k, l0, cnt in _CAND_BLOCKS:
        cnd = a[k:k + 1, :] + b[l0:l0 + 8, :]
        if cnt < 8:
            cnd = jnp.where(sub < cnt, cnd, -jnp.inf)
        cands.append(cnd)
        flats.append(sub + (k * PEER_TOPK + l0))
        valid.append(None if cnt == 8 else sub < cnt)
    cands.append(a[8:16, :] + b[0:1, :])
    flats.append((sub + 8) * PEER_TOPK)
    valid.append(None)
    return cands, flats, valid


def _counts_from_sels(sels):
    t = sels[0].shape[1]
    row16 = lax.broadcasted_iota(jnp.int32, (PEER_TOPK, t), 0)
    selfs = [s_.astype(F32) for s_ in sels]
    per_k = [jnp.sum(selfs[0] + selfs[1], axis=0, keepdims=True)]
    per_k += [jnp.sum(s_, axis=0, keepdims=True) for s_ in selfs[2:9]]
    counts = jnp.concatenate([jnp.zeros((8, t), F32), selfs[9]], axis=0)
    for k in range(8):
        counts = jnp.where(row16 == k, per_k[k], counts)
    return counts


def _pair_counts(a, b):
    t = a.shape[1]
    cands, flats, _ = _candidates(a, b)
    top = a[0:1, :] + b[0:1, :]
    sels = [jnp.zeros((8, t), jnp.bool_) for _ in cands]
    zsum = jnp.zeros((1, t), F32)
    big = PEER_TOPK * PEER_TOPK
    for _ in range(PEER_TOPK):
        m = functools.reduce(jnp.maximum, cands)
        m = jnp.max(m, axis=0, keepdims=True)
        idx = functools.reduce(jnp.minimum,
                               [jnp.where(c == m, f, big) for c, f in zip(cands, flats)])
        idx = jnp.min(idx, axis=0, keepdims=True)
        hit = [f == idx for f in flats]
        cands = [jnp.where(h_, -jnp.inf, c) for h_, c in zip(hit, cands)]
        sels = [jnp.logical_or(s_, h_) for s_, h_ in zip(sels, hit)]
        zsum = zsum + jnp.exp(m - top)
    return _counts_from_sels(sels), zsum


def _pair_counts_distinct(a, b):
    cands, _, valid = _candidates(a, b)
    top = a[0:1, :] + b[0:1, :]
    zsum = jnp.zeros((1, a.shape[1]), F32)
    for _ in range(PEER_TOPK):
        m = functools.reduce(jnp.maximum, cands)
        m = jnp.max(m, axis=0, keepdims=True)
        cands = [jnp.where(c == m, -jnp.inf, c) for c in cands]
        zsum = zsum + jnp.exp(m - top)
    sels = [c == -jnp.inf if v is None else jnp.logical_and(c == -jnp.inf, v)
            for c, v in zip(cands, valid)]
    counts = _counts_from_sels(sels)
    total = jnp.sum(counts, axis=0, keepdims=True)
    return counts, zsum, (total != float(PEER_TOPK)).astype(jnp.int32)


def _route_head(s1, s2, exact):
    if exact:
        rank1, a = _topk_rank(s1)
        rank2, b = _topk_rank(s2)
        counts, zsum = _pair_counts(a, b)
        bad = None
    else:
        rank1, a, bad1 = _topk_rank_distinct(s1)
        rank2, b, bad2 = _topk_rank_distinct(s2)
        counts, zsum, bad3 = _pair_counts_distinct(a, b)
        bad = bad1 + bad2 + bad3
    idx = rank1.astype(jnp.int32)
    sub = idx & (SUBLANES - 1)
    lo = jnp.take_along_axis(counts[:SUBLANES], sub, axis=0)
    hi = jnp.take_along_axis(counts[SUBLANES:], sub, axis=0)
    nn = jnp.where(idx < SUBLANES, lo, jnp.where(idx < PEER_TOPK, hi, 0.0))
    r2 = rank2.astype(F32).astype(BF16)
    e2 = jnp.exp(s2 - b[0:1, :]).astype(BF16)
    p = jnp.exp(s1 - a[0:1, :]) / zsum
    return (r2, e2, nn, p), bad


def _route_kernel(x_ref, g_ref, wq_ref, sk_ref, hb_ref, r2_ref, e2_ref, nn_ref, p_ref, sc_ref):
    hb = _rms(x_ref[...], g_ref[...]).astype(BF16)
    hb_ref[...] = hb
    qt = lax.dot_general(wq_ref[...], hb, (((1,), (1,)), ((), ())),
                         preferred_element_type=F32).astype(BF16)
    for hp in range(2 * PEER_HEADS):
        sc_ref[hp] = jnp.dot(sk_ref[hp], qt[hp * PEER_HALF:(hp + 1) * PEER_HALF],
                             preferred_element_type=F32)
    slabs = nn_ref.shape[1]

    def head(h, carry):
        s1 = sc_ref[2 * h]
        s2 = sc_ref[2 * h + 1]

        def store(vals):
            r2, e2, nn, p = vals
            r2_ref[h] = r2
            e2_ref[h] = e2
            for c in range(slabs):
                nn_ref[h, c] = nn[:, c * LANES:(c + 1) * LANES]
                p_ref[h, c] = p[:, c * LANES:(c + 1) * LANES]

        vals, bad = _route_head(s1, s2, exact=False)
        store(vals)

        @pl.when(jnp.max(bad) > 0)
        def _():
            store(_route_head(s1, s2, exact=True)[0])

        return carry

    lax.fori_loop(0, PEER_HEADS, head, 0)


def _route_call(x, g, wq_t, sk, *, tile=512):
    n, d = x.shape
    hk = (PEER_HEADS, PEER_KEYS, n)
    hs = (PEER_HEADS, n // LANES, PEER_KEYS, LANES)
    blk = pl.BlockSpec((PEER_HEADS, PEER_KEYS, tile), lambda i: (0, 0, i))
    slab = pl.BlockSpec((PEER_HEADS, tile // LANES, PEER_KEYS, LANES), lambda i: (0, i, 0, 0))
    return pl.pallas_call(
        _route_kernel,
        grid=(n // tile,),
        in_specs=[pl.BlockSpec((tile, d), lambda i: (i, 0)),
                  pl.BlockSpec((1, d), lambda i: (0, 0)),
                  pl.BlockSpec(wq_t.shape, lambda i: (0, 0)),
                  pl.BlockSpec(sk.shape, lambda i: (0, 0, 0))],
        out_specs=[pl.BlockSpec((tile, d), lambda i: (i, 0)), blk, blk, slab, slab],
        out_shape=[jax.ShapeDtypeStruct((n, d), BF16),
                   jax.ShapeDtypeStruct(hk, BF16), jax.ShapeDtypeStruct(hk, BF16),
                   jax.ShapeDtypeStruct(hs, F32), jax.ShapeDtypeStruct(hs, F32)],
        scratch_shapes=[pltpu.VMEM((2 * PEER_HEADS, PEER_KEYS, tile), F32)],
        compiler_params=_params("parallel"),
        name="peer_route",
    )(x, g, wq_t, sk)


def _gelu_sig(x):
    k0 = -2.0 * math.sqrt(2.0 / math.pi) * math.log2(math.e)
    return x / (1.0 + jnp.exp2(x * (k0 + (k0 * 0.044715) * (x * x))))


def _row_bf16(ref, h, i, rows):
    parts = [jnp.broadcast_to(ref[h, c, i:i + 1, :], (BF16_ROWS, LANES)).astype(BF16)
             for c in range(ref.shape[1])]
    row = jnp.concatenate(parts, axis=1)
    return jnp.broadcast_to(row[None], (rows // BF16_ROWS, BF16_ROWS, row.shape[1])).reshape(
        rows, row.shape[1])


def _peer_kernel(x_ref, hb_ref, u_ref, vt_ref, r2_ref, e2_ref, nn_ref, p_ref, gf_ref, o_ref,
                 act_ref, w_ref, acc_ref, *, final_norm, act_splits):
    et = pl.program_id(1)
    te = u_ref.shape[0]
    tm = hb_ref.shape[0]
    rows_i = te // PEER_KEYS

    @pl.when(et == 0)
    def _():
        acc_ref[...] = jnp.zeros_like(acc_ref)

    def gating(ii):
        rs = slice(ii * PEER_KEYS, (ii + 1) * PEER_KEYS)
        gate = jnp.zeros((PEER_KEYS, tm), BF16)
        for h in range(PEER_HEADS):
            nn = _row_bf16(nn_ref, h, ii, PEER_KEYS)
            p = _row_bf16(p_ref, h, ii, PEER_KEYS)
            gate = gate + e2_ref[h] * jnp.where(r2_ref[h] < nn, p, jnp.zeros_like(p))
        w_ref[rs, :] = gate * _gelu_sig(act_ref[rs, :].astype(BF16))

    mrows = te // act_splits
    for m in range(act_splits):
        rs = slice(m * mrows, (m + 1) * mrows)
        act_ref[rs, :] = lax.dot_general(u_ref[rs, :], hb_ref[...], (((1,), (1,)), ((), ())),
                                         preferred_element_type=F32)
        for ii in range(m * rows_i // act_splits, (m + 1) * rows_i // act_splits):
            gating(ii)
    acc_ref[...] += jnp.dot(vt_ref[...], w_ref[...], preferred_element_type=F32)

    @pl.when(et == pl.num_programs(1) - 1)
    def _():
        y = x_ref[...] + acc_ref[...].T
        if final_norm:
            y = _rms(y, gf_ref[...])
        o_ref[...] = y


def _peer_call(x, hb, u, vt, r2, e2, nn, p, g_final, *, final_norm, tm=512, te=2048,
               act_splits=4):
    n, d = x.shape
    rows_i = te // PEER_KEYS
    tok = pl.BlockSpec((tm, d), lambda t, e: (t, 0))
    allj = pl.BlockSpec((PEER_HEADS, PEER_KEYS, tm), lambda t, e: (0, 0, t))
    rowi = pl.BlockSpec((PEER_HEADS, tm // LANES, rows_i, LANES), lambda t, e: (0, t, e, 0))
    return pl.pallas_call(
        functools.partial(_peer_kernel, final_norm=final_norm, act_splits=act_splits),
        grid=(n // tm, u.shape[0] // te),
        in_specs=[tok, tok,
                  pl.BlockSpec((te, d), lambda t, e: (e, 0)),
                  pl.BlockSpec((d, te), lambda t, e: (0, e)),
                  allj, allj, rowi, rowi,
                  pl.BlockSpec((1, d), lambda t, e: (0, 0))],
        out_specs=tok,
        out_shape=jax.ShapeDtypeStruct((n, d), F32),
        scratch_shapes=[pltpu.VMEM((te, tm), F32), pltpu.VMEM((te, tm), BF16),
                        pltpu.VMEM((d, tm), F32)],
        compiler_params=_params("parallel", "arbitrary"),
        name="peer_dense",
    )(x, hb, u, vt, r2, e2, nn, p, g_final)


def _peer_layer(x, g, w_q, sub_keys, u_tab, v_tab, g_final, final_norm):
    wq_t = w_q.T.astype(BF16)
    sk = sub_keys.reshape(PEER_HEADS * 2, PEER_KEYS, PEER_HALF).astype(BF16)
    hb, r2, e2, nn, p = _route_call(x, g, wq_t, sk)
    return _peer_call(x, hb, u_tab.astype(BF16), v_tab.T.astype(BF16), r2, e2, nn, p, g_final,
                      final_norm=final_norm)


def _rope_tables(positions):
    half = HEAD_DIM // 2
    inv = ROPE_THETA ** (-jnp.arange(half, dtype=F32) / half)
    ang = positions.astype(F32)[:, None] * inv[None, :]
    cos = jnp.tile(jnp.cos(ang), (1, 4))
    sin = jnp.sin(ang)
    return cos, jnp.concatenate([-sin, -sin, sin, sin], axis=1)


def _attention_layer(x, g, w_qkv, w_o, *, batch, seq):
    n, d = x.shape
    perm = _rope_perm(d // HEAD_DIM)
    cos, sin = _rope_tables(_band_positions(batch, seq))
    wg = w_qkv.reshape(d, len(DILATED_GROUPS), 3, d)
    outs, lses = [], []
    for gi, (window, dilation) in enumerate(DILATED_GROUPS):
        assert window // dilation == ATTN_BLOCK and 16 % dilation == 0
        w = jnp.concatenate([wg[:, gi, 0][:, perm], wg[:, gi, 1][:, perm], wg[:, gi, 2]],
                            axis=1).astype(BF16)
        dtype = BF16 if (ATTN_BLOCK * dilation // 16) % 16 == 0 else F32
        q, k, v = _qkv_call(x, g, w, cos, sin, dtype=dtype)
        o, lse = _attn_call(q, k, v, dilation=dilation, batch=batch, seq=seq)
        outs.append(o)
        lses.append(lse)
    return _attn_out_call(x, outs, lses, w_o.astype(BF16))


def kernel(x, norm_mix, norm_ffn, norm_final, s5_lam_re, s5_lam_im, s5_log_step, s5_b_re, s5_b_im, s5_c_re, s5_c_im, s5_d, s5_w_glu, attn_w_qkv, attn_w_o, peer_w_q, peer_sub_keys, peer_u, peer_v):
    batch, seq, d = x.shape
    assert batch == 8, "one timestep of all batches must fill one 8-sublane group"
    assert seq % BAND_MACRO == 0
    depth = norm_mix.shape[0]
    xs = x.transpose(1, 0, 2).reshape(seq * batch, d)
    band = False
    g_final = norm_final.reshape(1, d)
    for i in range(depth):
        j = i // 2
        g_mix = norm_mix[i].reshape(1, d)
        if i % 2 == 0:
            if band:
                xs = _from_band_order(xs, batch, seq).transpose(1, 0, 2).reshape(seq * batch, d)
                band = False
            wbu, are, aim, wc = _s5_weights(s5_lam_re[j], s5_lam_im[j], s5_log_step[j],
                                            s5_b_re[j], s5_b_im[j], s5_c_re[j], s5_c_im[j])
            y = _s5_call(xs, g_mix, wbu, are, aim, wc, s5_d[j].reshape(1, d), batch=batch)
            xs = _glu_call(y, xs, s5_w_glu[j].astype(BF16))
        else:
            if not band:
                xs = _to_band_order(xs, batch, seq)
                band = True
            xs = _attention_layer(xs, g_mix, attn_w_qkv[j], attn_w_o[j], batch=batch, seq=seq)
        xs = _peer_layer(xs, norm_ffn[i].reshape(1, d), peer_w_q[i], peer_sub_keys[i],
                         peer_u[i], peer_v[i], g_final, final_norm=(i == depth - 1))
    if band:
        return _from_band_order(xs, batch, seq)
    return xs.reshape(seq, batch, d).transpose(1, 0, 2)
```

```python
import functools
import math

import jax
import jax.numpy as jnp
import numpy as np
from jax import lax
from jax.experimental import pallas as pl
from jax.experimental.pallas import tpu as pltpu

F32 = jnp.float32
BF16 = jnp.bfloat16

RMS_EPS = 1e-6
SSM_GROUP = 16
SSM_STATE = 64
SSM_BLOCK_GROUPS = 16
HEAD_DIM = 64
DILATED_GROUPS = ((128, 1), (512, 4), (2048, 16))
ATTN_BLOCK = 128
ROPE_THETA = 10000.0
PEER_HEADS = 8
PEER_KEYS = 128
PEER_HALF = 128
PEER_TOPK = 16
NEG_BIG = -1e30

LANES = 128
SUBLANES = 8
BF16_ROWS = 16
MXU_WIDTH = 256
VMEM_LIMIT_BYTES = 56 * 1024 * 1024


def _params(*sem):
    return pltpu.CompilerParams(dimension_semantics=sem, vmem_limit_bytes=VMEM_LIMIT_BYTES)


def _rms(x, g):
    return x * lax.rsqrt(jnp.mean(x * x, axis=-1, keepdims=True) + RMS_EPS) * g


def _gelu(x):
    c = math.sqrt(2.0 / math.pi)
    return 0.5 * x * (1.0 + jnp.tanh(c * (x + 0.044715 * (x * x * x))))


def _s5_kernel(x_ref, g_ref, wbu_ref, are_ref, aim_ref, wc_ref, d_ref, y_ref, bu_ref, st_ref,
               *, batch, nblk):
    @pl.when(pl.program_id(0) == 0)
    def _():
        st_ref[...] = jnp.zeros_like(st_ref)

    rows = x_ref.shape[0]
    steps = rows // batch
    h = _rms(x_ref[...], g_ref[...])
    hb = h.astype(BF16)
    kin = wbu_ref.shape[1]
    half = wbu_ref.shape[2] // 2
    for c in range(nblk):
        bu_ref[...] = jnp.dot(hb[:, c * kin:(c + 1) * kin], wbu_ref[c],
                              preferred_element_type=F32)
        are = jnp.broadcast_to(are_ref[c], (batch, half))
        aim = jnp.broadcast_to(aim_ref[c], (batch, half))

        def step(t, carry):
            sre, sim = carry
            r = pl.multiple_of(t * batch, batch)
            bre = bu_ref[pl.ds(r, batch), 0:half]
            bim = bu_ref[pl.ds(r, batch), half:2 * half]
            nre = are * sre - aim * sim + bre
            nim = are * sim + aim * sre + bim
            bu_ref[pl.ds(r, batch), 0:half] = nre
            bu_ref[pl.ds(r, batch), half:2 * half] = nim
            return nre, nim

        sre, sim = lax.fori_loop(0, steps, step,
                                 (st_ref[c, :, 0:half], st_ref[c, :, half:2 * half]),
                                 unroll=True)
        st_ref[c, :, 0:half] = sre
        st_ref[c, :, half:2 * half] = sim
        yc = jnp.dot(bu_ref[...].astype(BF16), wc_ref[c], preferred_element_type=F32)
        yc = yc + d_ref[:, c * kin:(c + 1) * kin] * h[:, c * kin:(c + 1) * kin]
        y_ref[:, c * kin:(c + 1) * kin] = _gelu(yc).astype(BF16)


def _s5_weights(lam_re, lam_im, log_step, b_re, b_im, c_re, c_im):
    G, P = lam_re.shape
    H = b_re.shape[-1]
    step = jnp.exp(log_step)[:, None]
    mag = jnp.exp(lam_re * step)
    lb_re = mag * jnp.cos(lam_im * step)
    lb_im = mag * jnp.sin(lam_im * step)
    den = lam_re * lam_re + lam_im * lam_im
    num_re = lb_re - 1.0
    coef_re = (num_re * lam_re + lb_im * lam_im) / den
    coef_im = (lb_im * lam_re - num_re * lam_im) / den
    bb_re = coef_re[..., None] * b_re - coef_im[..., None] * b_im
    bb_im = coef_re[..., None] * b_im + coef_im[..., None] * b_re
    gb = SSM_BLOCK_GROUPS
    nblk = G // gb
    eye = jnp.eye(gb, dtype=F32)

    def bdiag_in(w):
        w = w.reshape(nblk, gb, P, H)
        return jnp.einsum('cgph,gk->cghkp', w, eye).reshape(nblk, gb * H, gb * P)

    def bdiag_out(w):
        w = w.reshape(nblk, gb, H, P)
        return jnp.einsum('cghp,gk->cgpkh', w, eye).reshape(nblk, gb * P, gb * H)

    wbu = jnp.concatenate([bdiag_in(bb_re), bdiag_in(bb_im)], axis=-1).astype(BF16)
    wc = jnp.concatenate([bdiag_out(c_re), -bdiag_out(c_im)], axis=1).astype(BF16)
    are = lb_re.reshape(nblk, 1, gb * P)
    aim = lb_im.reshape(nblk, 1, gb * P)
    return wbu, are, aim, wc


def _s5_call(x, g, wbu, are, aim, wc, d_skip, *, batch, steps_per_tile=64):
    n, d = x.shape
    nblk = wbu.shape[0]
    rows = batch * steps_per_tile
    assert n % rows == 0
    const3 = lambda i: (0, 0, 0)
    const2 = lambda i: (0, 0)
    return pl.pallas_call(
        functools.partial(_s5_kernel, batch=batch, nblk=nblk),
        grid=(n // rows,),
        in_specs=[pl.BlockSpec((rows, d), lambda i: (i, 0)),
                  pl.BlockSpec((1, d), const2),
                  pl.BlockSpec(wbu.shape, const3),
                  pl.BlockSpec(are.shape, const3),
                  pl.BlockSpec(aim.shape, const3),
                  pl.BlockSpec(wc.shape, const3),
                  pl.BlockSpec((1, d), const2)],
        out_specs=pl.BlockSpec((rows, d), lambda i: (i, 0)),
        out_shape=jax.ShapeDtypeStruct((n, d), BF16),
        scratch_shapes=[pltpu.VMEM((rows, wbu.shape[2]), F32),
                        pltpu.VMEM((nblk, batch, wbu.shape[2]), F32)],
        compiler_params=_params("arbitrary"),
        name="s5_ssm",
    )(x, g, wbu, are, aim, wc, d_skip)


def _glu_kernel(y_ref, x_ref, w_ref, o_ref):
    z = jnp.dot(y_ref[...], w_ref[...], preferred_element_type=F32)
    d = o_ref.shape[1]
    o_ref[...] = x_ref[...] + z[:, :d] * jax.nn.sigmoid(z[:, d:])


def _glu_call(y, x, w, *, tile=512):
    n, d = x.shape
    return pl.pallas_call(
        _glu_kernel,
        grid=(n // tile,),
        in_specs=[pl.BlockSpec((tile, d), lambda i: (i, 0)),
                  pl.BlockSpec((tile, d), lambda i: (i, 0)),
                  pl.BlockSpec(w.shape, lambda i: (0, 0))],
        out_specs=pl.BlockSpec((tile, d), lambda i: (i, 0)),
        out_shape=jax.ShapeDtypeStruct((n, d), F32),
        compiler_params=_params("parallel"),
        name="s5_glu",
    )(y, x, w)


def _rope_perm(n_heads):
    half = HEAD_DIM // 2
    idx = []
    for p in range(n_heads // 2):
        h0, h1 = 2 * p, 2 * p + 1
        for part in (0, 1):
            for h in (h0, h1):
                idx.extend(range(h * HEAD_DIM + part * half, h * HEAD_DIM + (part + 1) * half))
    return np.asarray(idx, dtype=np.int32)


BAND_MACRO = ATTN_BLOCK * 16


def _to_band_order(x, batch, seq):
    d = x.shape[1]
    x = x.reshape(seq // BAND_MACRO, ATTN_BLOCK, 16, batch, d)
    return x.transpose(0, 2, 3, 1, 4).reshape(seq * batch, d)


def _from_band_order(x, batch, seq):
    d = x.shape[1]
    x = x.reshape(seq // BAND_MACRO, 16, batch, ATTN_BLOCK, d)
    return x.transpose(2, 0, 3, 1, 4).reshape(batch, seq, d)


def _band_positions(batch, seq):
    macro = jnp.arange(seq // BAND_MACRO, dtype=jnp.int32)[:, None, None, None]
    r16 = jnp.arange(16, dtype=jnp.int32)[None, :, None, None]
    ml = jnp.arange(ATTN_BLOCK, dtype=jnp.int32)[None, None, None, :]
    t = macro * BAND_MACRO + ml * 16 + r16
    return jnp.broadcast_to(t, (seq // BAND_MACRO, 16, batch, ATTN_BLOCK)).reshape(-1)


def _qkv_kernel(x_ref, g_ref, w_ref, cos_ref, sin_ref, q_ref, k_ref, v_ref):
    d = x_ref.shape[1]
    hb = _rms(x_ref[...], g_ref[...]).astype(BF16)
    cos = cos_ref[...]
    sin = sin_ref[...]
    scale = HEAD_DIM ** -0.5

    def roped(col0, ref, mul):
        for c in range(d // MXU_WIDTH):
            both = jnp.dot(hb, w_ref[:, col0 + c * MXU_WIDTH:col0 + (c + 1) * MXU_WIDTH],
                           preferred_element_type=F32)
            for half in range(MXU_WIDTH // LANES):
                blk = both[:, half * LANES:(half + 1) * LANES]
                out = blk * cos + pltpu.roll(blk, HEAD_DIM, axis=1) * sin
                if mul != 1.0:
                    out = out * mul
                lo = c * MXU_WIDTH + half * LANES
                ref[:, lo:lo + LANES] = out.astype(ref.dtype)

    roped(0, q_ref, scale)
    roped(d, k_ref, 1.0)
    v_ref[...] = jnp.dot(hb, w_ref[:, 2 * d:3 * d],
                         preferred_element_type=F32).astype(v_ref.dtype)


def _qkv_call(x, g, w, cos, sin, *, dtype, tile=512):
    n, d = x.shape
    row = pl.BlockSpec((tile, d), lambda i: (i, 0))
    tab = pl.BlockSpec((tile, LANES), lambda i: (i, 0))
    out = jax.ShapeDtypeStruct((n, d), dtype)
    return pl.pallas_call(
        _qkv_kernel,
        grid=(n // tile,),
        in_specs=[row, pl.BlockSpec((1, d), lambda i: (0, 0)),
                  pl.BlockSpec(w.shape, lambda i: (0, 0)), tab, tab],
        out_specs=[row, row, row],
        out_shape=[out, out, out],
        compiler_params=_params("parallel"),
        name="attn_qkv",
    )(x, g, w, cos, sin)


ATTN_CLASSES_PER_STEP = 4


def _attn_kernel(q_ref, kp_ref, kc_ref, vp_ref, vc_ref, o_ref, l_ref, *, chunk):
    c = ATTN_BLOCK
    d = q_ref.shape[-1]
    first = pl.program_id(1) == 0

    def pos(r):
        return (r % chunk) * (c // chunk) + r // chunk

    qrow = lax.broadcasted_iota(jnp.int32, (c, 2 * c), 0)
    kcol = lax.broadcasted_iota(jnp.int32, (c, 2 * c), 1)
    cur = kcol >= c
    kpos = jnp.where(cur, pos(kcol - c) + c, pos(kcol))
    dist = pos(qrow) + c - kpos
    valid = (dist >= 0) & (dist <= c) & jnp.logical_or(cur, jnp.logical_not(first))
    lane = lax.broadcasted_iota(jnp.int32, (c, LANES), 1)
    qmask0 = (lane // (HEAD_DIM // 2)) % 2 == 0
    omask0 = lane < HEAD_DIM
    blk = (q_ref.shape[0], chunk, LANES)

    def load(ref, cl, sl):
        return ref[:, cl, :, sl].reshape(c, LANES).astype(BF16)

    for cl in range(q_ref.shape[1]):
        lse = jnp.zeros((c, LANES), F32)
        for p in range(d // LANES):
            sl = slice(p * LANES, (p + 1) * LANES)
            qp = load(q_ref, cl, sl)
            kp = jnp.concatenate([load(kp_ref, cl, sl), load(kc_ref, cl, sl)], axis=0)
            vp = jnp.concatenate([load(vp_ref, cl, sl), load(vc_ref, cl, sl)], axis=0)
            outs = []
            for e in range(2):
                qm = jnp.where(qmask0 if e == 0 else jnp.logical_not(qmask0), qp,
                               jnp.zeros_like(qp))
                s = lax.dot_general(qm, kp, (((1,), (1,)), ((), ())),
                                    preferred_element_type=F32)
                s = jnp.where(valid, s, NEG_BIG)
                smax = jnp.max(s, axis=-1, keepdims=True)
                ex = jnp.exp(s - smax)
                den = jnp.sum(ex, axis=-1, keepdims=True)
                o = jnp.dot(ex.astype(BF16), vp, preferred_element_type=F32)
                outs.append(o / den)
                lse = jnp.where(lane == 2 * p + e, smax + jnp.log(den), lse)
            o_ref[:, cl, :, sl] = jnp.where(omask0, outs[0], outs[1]).astype(
                o_ref.dtype).reshape(blk)
        l_ref[:, cl] = lse.reshape(blk)


def _attn_call(q, k, v, *, dilation, batch, seq):
    n, d = q.shape
    c = ATTN_BLOCK
    macros = seq // BAND_MACRO
    per = 16 // dilation
    chunk = c // per
    ncls = ATTN_CLASSES_PER_STEP
    view = (macros, per, dilation * batch, per, chunk, d)
    block = (None, per, ncls, None, chunk, d)

    def at(s, i):
        return (i // per, 0, s, i % per, 0, 0)

    cur = pl.BlockSpec(block, at)
    prev = pl.BlockSpec(block, lambda s, i: at(s, jnp.maximum(i - 1, 0)))
    lse_view = view[:-1] + (LANES,)
    o, lse = pl.pallas_call(
        functools.partial(_attn_kernel, chunk=chunk),
        grid=(dilation * batch // ncls, macros * per),
        in_specs=[cur, prev, cur, prev, cur],
        out_specs=[cur, pl.BlockSpec(block[:-1] + (LANES,), at)],
        out_shape=[jax.ShapeDtypeStruct(view, q.dtype), jax.ShapeDtypeStruct(lse_view, F32)],
        compiler_params=_params("parallel", "parallel"),
        name="attn_band",
    )(q.reshape(view), k.reshape(view), k.reshape(view), v.reshape(view), v.reshape(view))
    return o.reshape(n, d), lse.reshape(n, LANES)


def _attn_out_kernel(x_ref, o0, o1, o2, l0, l1, l2, e_ref, w_ref, y_ref):
    a, b, c = l0[...], l1[...], l2[...]
    mx = jnp.maximum(jnp.maximum(a, b), c)
    ea, eb, ec = jnp.exp(a - mx), jnp.exp(b - mx), jnp.exp(c - mx)
    inv = 1.0 / (ea + eb + ec)

    def spread(wt):
        hi = wt.astype(BF16)
        lo = (wt - hi.astype(F32)).astype(BF16)
        return (jnp.dot(hi, e_ref[...], preferred_element_type=F32)
                + jnp.dot(lo, e_ref[...], preferred_element_type=F32))

    o = spread(ea * inv) * o0[...].astype(F32) + spread(eb * inv) * o1[...].astype(F32) \
        + spread(ec * inv) * o2[...].astype(F32)
    y_ref[...] = x_ref[...] + jnp.dot(o.astype(BF16), w_ref[...], preferred_element_type=F32)


def _attn_out_call(x, os_, ls_, w, *, tile=512):
    n, d = x.shape
    row = pl.BlockSpec((tile, d), lambda i: (i, 0))
    lrow = pl.BlockSpec((tile, LANES), lambda i: (i, 0))
    heads = jnp.arange(LANES, dtype=jnp.int32)[:, None]
    spread = (jnp.arange(d, dtype=jnp.int32)[None, :] // HEAD_DIM == heads).astype(BF16)
    return pl.pallas_call(
        _attn_out_kernel,
        grid=(n // tile,),
        in_specs=[row] * 4 + [lrow] * 3 + [pl.BlockSpec(spread.shape, lambda i: (0, 0)),
                                           pl.BlockSpec(w.shape, lambda i: (0, 0))],
        out_specs=row,
        out_shape=jax.ShapeDtypeStruct((n, d), F32),
        compiler_params=_params("parallel"),
        name="attn_out",
    )(x, *os_, *ls_, spread, w)


def _topk_rank(s):
    nk, t = s.shape
    iota = lax.broadcasted_iota(jnp.int32, (nk, t), 0)
    row16 = lax.broadcasted_iota(jnp.int32, (PEER_TOPK, t), 0)
    rank = jnp.full((nk, t), 99, jnp.int32)
    vals = jnp.zeros((PEER_TOPK, t), F32)
    for k in range(PEER_TOPK):
        m = jnp.max(s, axis=0, keepdims=True)
        idx = jnp.min(jnp.where(s == m, iota, nk), axis=0, keepdims=True)
        sel = iota == idx
        rank = jnp.where(sel, k, rank)
        s = jnp.where(sel, -jnp.inf, s)
        vals = jnp.where(row16 == k, m, vals)
    return rank, vals


SENTINEL_BASE = 1e38
SENTINEL_STEP = 1e37
SENTINEL_LIMIT = -0.95e38


def _topk_rank_distinct(s):
    nk, t = s.shape
    row16 = lax.broadcasted_iota(jnp.int32, (PEER_TOPK, t), 0)
    low = jnp.min(s, axis=0, keepdims=True)
    vals = jnp.zeros((PEER_TOPK, t), F32)
    for k in range(PEER_TOPK):
        m = jnp.max(s, axis=0, keepdims=True)
        s = jnp.where(s == m, -(SENTINEL_BASE + k * SENTINEL_STEP), s)
        vals = jnp.where(row16 == k, m, vals)
    top = s < SENTINEL_LIMIT
    rank = jnp.where(top, jnp.floor(s * (-1.0 / SENTINEL_STEP) - (SENTINEL_BASE / SENTINEL_STEP - 0.5)),
                     99.0)
    cnt = jnp.sum(top.astype(F32), axis=0, keepdims=True)
    bad = jnp.logical_or(cnt != float(PEER_TOPK), jnp.logical_not(low > SENTINEL_LIMIT))
    return rank, vals, bad.astype(jnp.int32)


_CAND_BLOCKS = ((0, 0, 8), (0, 8, 8), (1, 0, 8), (2, 0, 5), (3, 0, 4), (4, 0, 3), (5, 0, 2),
                (6, 0, 2), (7, 0, 2))


def _candidates(a, b):
    t = a.shape[1]
    sub = lax.broadcasted_iota(jnp.int32, (8, t), 0)
    cands, flats, valid = [], [], []
    for k, l0, cnt in _CAND_BLOCKS:
        cnd = a[k:k + 1, :] + b[l0:l0 + 8, :]
        if cnt < 8:
            cnd = jnp.where(sub < cnt, cnd, -jnp.inf)
        cands.append(cnd)
        flats.append(sub + (k * PEER_TOPK + l0))
        valid.append(None if cnt == 8 else sub < cnt)
    cands.append(a[8:16, :] + b[0:1, :])
    flats.append((sub + 8) * PEER_TOPK)
    valid.append(None)
    return cands, flats, valid


def _counts_from_sels(sels):
    t = sels[0].shape[1]
    row16 = lax.broadcasted_iota(jnp.int32, (PEER_TOPK, t), 0)
    selfs = [s_.astype(F32) for s_ in sels]
    per_k = [jnp.sum(selfs[0] + selfs[1], axis=0, keepdims=True)]
    per_k += [jnp.sum(s_, axis=0, keepdims=True) for s_ in selfs[2:9]]
    counts = jnp.concatenate([jnp.zeros((8, t), F32), selfs[9]], axis=0)
    for k in range(8):
        counts = jnp.where(row16 == k, per_k[k], counts)
    return counts


def _pair_counts(a, b):
    t = a.shape[1]
    cands, flats, _ = _candidates(a, b)
    top = a[0:1, :] + b[0:1, :]
    sels = [jnp.zeros((8, t), jnp.bool_) for _ in cands]
    zsum = jnp.zeros((1, t), F32)
    big = PEER_TOPK * PEER_TOPK
    for _ in range(PEER_TOPK):
        m = functools.reduce(jnp.maximum, cands)
        m = jnp.max(m, axis=0, keepdims=True)
        idx = functools.reduce(jnp.minimum,
                               [jnp.where(c == m, f, big) for c, f in zip(cands, flats)])
        idx = jnp.min(idx, axis=0, keepdims=True)
        hit = [f == idx for f in flats]
        cands = [jnp.where(h_, -jnp.inf, c) for h_, c in zip(hit, cands)]
        sels = [jnp.logical_or(s_, h_) for s_, h_ in zip(sels, hit)]
        zsum = zsum + jnp.exp(m - top)
    return _counts_from_sels(sels), zsum


def _pair_counts_distinct(a, b):
    cands, _, valid = _candidates(a, b)
    top = a[0:1, :] + b[0:1, :]
    zsum = jnp.zeros((1, a.shape[1]), F32)
    for _ in range(PEER_TOPK):
        m = functools.reduce(jnp.maximum, cands)
        m = jnp.max(m, axis=0, keepdims=True)
        cands = [jnp.where(c == m, -jnp.inf, c) for c in cands]
        zsum = zsum + jnp.exp(m - top)
    sels = [c == -jnp.inf if v is None else jnp.logical_and(c == -jnp.inf, v)
            for c, v in zip(cands, valid)]
    counts = _counts_from_sels(sels)
    total = jnp.sum(counts, axis=0, keepdims=True)
    return counts, zsum, (total != float(PEER_TOPK)).astype(jnp.int32)


def _route_head(s1, s2, exact):
    if exact:
        rank1, a = _topk_rank(s1)
        rank2, b = _topk_rank(s2)
        counts, zsum = _pair_counts(a, b)
        bad = None
    else:
        rank1, a, bad1 = _topk_rank_distinct(s1)
        rank2, b, bad2 = _topk_rank_distinct(s2)
        counts, zsum, bad3 = _pair_counts_distinct(a, b)
        bad = bad1 + bad2 + bad3
    idx = rank1.astype(jnp.int32)
    sub = idx & (SUBLANES - 1)
    lo = jnp.take_along_axis(counts[:SUBLANES], sub, axis=0)
    hi = jnp.take_along_axis(counts[SUBLANES:], sub, axis=0)
    nn = jnp.where(idx < SUBLANES, lo, jnp.where(idx < PEER_TOPK, hi, 0.0))
    r2 = rank2.astype(F32).astype(BF16)
    e2 = jnp.exp(s2 - b[0:1, :]).astype(BF16)
    p = jnp.exp(s1 - a[0:1, :]) / zsum
    return (r2, e2, nn, p), bad


def _route_kernel(x_ref, g_ref, wq_ref, sk_ref, hb_ref, r2_ref, e2_ref, nn_ref, p_ref, sc_ref):
    hb = _rms(x_ref[...], g_ref[...]).astype(BF16)
    hb_ref[...] = hb
    qt = lax.dot_general(wq_ref[...], hb, (((1,), (1,)), ((), ())),
                         preferred_element_type=F32).astype(BF16)
    for hp in range(2 * PEER_HEADS):
        sc_ref[hp] = jnp.dot(sk_ref[hp], qt[hp * PEER_HALF:(hp + 1) * PEER_HALF],
                             preferred_element_type=F32)
    slabs = nn_ref.shape[1]

    def head(h, carry):
        s1 = sc_ref[2 * h]
        s2 = sc_ref[2 * h + 1]

        def store(vals):
            r2, e2, nn, p = vals
            r2_ref[h] = r2
            e2_ref[h] = e2
            for c in range(slabs):
                nn_ref[h, c] = nn[:, c * LANES:(c + 1) * LANES]
                p_ref[h, c] = p[:, c * LANES:(c + 1) * LANES]

        vals, bad = _route_head(s1, s2, exact=False)
        store(vals)

        @pl.when(jnp.max(bad) > 0)
        def _():
            store(_route_head(s1, s2, exact=True)[0])

        return carry

    lax.fori_loop(0, PEER_HEADS, head, 0)


def _route_call(x, g, wq_t, sk, *, tile=512):
    n, d = x.shape
    hk = (PEER_HEADS, PEER_KEYS, n)
    hs = (PEER_HEADS, n // LANES, PEER_KEYS, LANES)
    blk = pl.BlockSpec((PEER_HEADS, PEER_KEYS, tile), lambda i: (0, 0, i))
    slab = pl.BlockSpec((PEER_HEADS, tile // LANES, PEER_KEYS, LANES), lambda i: (0, i, 0, 0))
    return pl.pallas_call(
        _route_kernel,
        grid=(n // tile,),
        in_specs=[pl.BlockSpec((tile, d), lambda i: (i, 0)),
                  pl.BlockSpec((1, d), lambda i: (0, 0)),
                  pl.BlockSpec(wq_t.shape, lambda i: (0, 0)),
                  pl.BlockSpec(sk.shape, lambda i: (0, 0, 0))],
        out_specs=[pl.BlockSpec((tile, d), lambda i: (i, 0)), blk, blk, slab, slab],
        out_shape=[jax.ShapeDtypeStruct((n, d), BF16),
                   jax.ShapeDtypeStruct(hk, BF16), jax.ShapeDtypeStruct(hk, BF16),
                   jax.ShapeDtypeStruct(hs, F32), jax.ShapeDtypeStruct(hs, F32)],
        scratch_shapes=[pltpu.VMEM((2 * PEER_HEADS, PEER_KEYS, tile), F32)],
        compiler_params=_params("parallel"),
        name="peer_route",
    )(x, g, wq_t, sk)


def _gelu_sig(x):
    k0 = -2.0 * math.sqrt(2.0 / math.pi) * math.log2(math.e)
    return x / (1.0 + jnp.exp2(x * (k0 + (k0 * 0.044715) * (x * x))))


def _row_bf16(ref, h, i, rows):
    parts = [jnp.broadcast_to(ref[h, c, i:i + 1, :], (BF16_ROWS, LANES)).astype(BF16)
             for c in range(ref.shape[1])]
    row = jnp.concatenate(parts, axis=1)
    return jnp.broadcast_to(row[None], (rows // BF16_ROWS, BF16_ROWS, row.shape[1])).reshape(
        rows, row.shape[1])


def _peer_kernel(x_ref, hb_ref, u_ref, vt_ref, r2_ref, e2_ref, nn_ref, p_ref, gf_ref, o_ref,
                 act_ref, w_ref, acc_ref, *, final_norm, act_splits):
    et = pl.program_id(1)
    te = u_ref.shape[0]
    tm = hb_ref.shape[0]
    rows_i = te // PEER_KEYS

    @pl.when(et == 0)
    def _():
        acc_ref[...] = jnp.zeros_like(acc_ref)

    def gating(ii):
        rs = slice(ii * PEER_KEYS, (ii + 1) * PEER_KEYS)
        gate = jnp.zeros((PEER_KEYS, tm), BF16)
        for h in range(PEER_HEADS):
            nn = _row_bf16(nn_ref, h, ii, PEER_KEYS)
            p = _row_bf16(p_ref, h, ii, PEER_KEYS)
            gate = gate + e2_ref[h] * jnp.where(r2_ref[h] < nn, p, jnp.zeros_like(p))
        w_ref[rs, :] = gate * _gelu_sig(act_ref[rs, :].astype(BF16))

    mrows = te // act_splits
    for m in range(act_splits):
        rs = slice(m * mrows, (m + 1) * mrows)
        act_ref[rs, :] = lax.dot_general(u_ref[rs, :], hb_ref[...], (((1,), (1,)), ((), ())),
                                         preferred_element_type=F32)
        for ii in range(m * rows_i // act_splits, (m + 1) * rows_i // act_splits):
            gating(ii)
    acc_ref[...] += jnp.dot(vt_ref[...], w_ref[...], preferred_element_type=F32)

    @pl.when(et == pl.num_programs(1) - 1)
    def _():
        y = x_ref[...] + acc_ref[...].T
        if final_norm:
            y = _rms(y, gf_ref[...])
        o_ref[...] = y


def _peer_call(x, hb, u, vt, r2, e2, nn, p, g_final, *, final_norm, tm=512, te=2048,
               act_splits=4):
    n, d = x.shape
    rows_i = te // PEER_KEYS
    tok = pl.BlockSpec((tm, d), lambda t, e: (t, 0))
    allj = pl.BlockSpec((PEER_HEADS, PEER_KEYS, tm), lambda t, e: (0, 0, t))
    rowi = pl.BlockSpec((PEER_HEADS, tm // LANES, rows_i, LANES), lambda t, e: (0, t, e, 0))
    return pl.pallas_call(
        functools.partial(_peer_kernel, final_norm=final_norm, act_splits=act_splits),
        grid=(n // tm, u.shape[0] // te),
        in_specs=[tok, tok,
                  pl.BlockSpec((te, d), lambda t, e: (e, 0)),
                  pl.BlockSpec((d, te), lambda t, e: (0, e)),
                  allj, allj, rowi, rowi,
                  pl.BlockSpec((1, d), lambda t, e: (0, 0))],
        out_specs=tok,
        out_shape=jax.ShapeDtypeStruct((n, d), F32),
        scratch_shapes=[pltpu.VMEM((te, tm), F32), pltpu.VMEM((te, tm), BF16),
                        pltpu.VMEM((d, tm), F32)],
        compiler_params=_params("parallel", "arbitrary"),
        name="peer_dense",
    )(x, hb, u, vt, r2, e2, nn, p, g_final)


def _peer_layer(x, g, w_q, sub_keys, u_tab, v_tab, g_final, final_norm):
    wq_t = w_q.T.astype(BF16)
    sk = sub_keys.reshape(PEER_HEADS * 2, PEER_KEYS, PEER_HALF).astype(BF16)
    hb, r2, e2, nn, p = _route_call(x, g, wq_t, sk)
    return _peer_call(x, hb, u_tab.astype(BF16), v_tab.T.astype(BF16), r2, e2, nn, p, g_final,
                      final_norm=final_norm)


def _rope_tables(positions):
    half = HEAD_DIM // 2
    inv = ROPE_THETA ** (-jnp.arange(half, dtype=F32) / half)
    ang = positions.astype(F32)[:, None] * inv[None, :]
    cos = jnp.tile(jnp.cos(ang), (1, 4))
    sin = jnp.sin(ang)
    return cos, jnp.concatenate([-sin, -sin, sin, sin], axis=1)


def _attention_layer(x, g, w_qkv, w_o, *, batch, seq):
    n, d = x.shape
    perm = _rope_perm(d // HEAD_DIM)
    cos, sin = _rope_tables(_band_positions(batch, seq))
    wg = w_qkv.reshape(d, len(DILATED_GROUPS), 3, d)
    outs, lses = [], []
    for gi, (window, dilation) in enumerate(DILATED_GROUPS):
        assert window // dilation == ATTN_BLOCK and 16 % dilation == 0
        w = jnp.concatenate([wg[:, gi, 0][:, perm], wg[:, gi, 1][:, perm], wg[:, gi, 2]],
                            axis=1).astype(BF16)
        dtype = BF16 if (ATTN_BLOCK * dilation // 16) % 16 == 0 else F32
        q, k, v = _qkv_call(x, g, w, cos, sin, dtype=dtype)
        o, lse = _attn_call(q, k, v, dilation=dilation, batch=batch, seq=seq)
        outs.append(o)
        lses.append(lse)
    return _attn_out_call(x, outs, lses, w_o.astype(BF16))


def kernel(x, norm_mix, norm_ffn, norm_final, s5_lam_re, s5_lam_im, s5_log_step, s5_b_re, s5_b_im, s5_c_re, s5_c_im, s5_d, s5_w_glu, attn_w_qkv, attn_w_o, peer_w_q, peer_sub_keys, peer_u, peer_v):
    batch, seq, d = x.shape
    assert batch == 8, "one timestep of all batches must fill one 8-sublane group"
    assert seq % BAND_MACRO == 0
    depth = norm_mix.shape[0]
    xs = x.transpose(1, 0, 2).reshape(seq * batch, d)
    band = False
    g_final = norm_final.reshape(1, d)
    for i in range(depth):
        j = i // 2
        g_mix = norm_mix[i].reshape(1, d)
        if i % 2 == 0:
            if band:
                xs = _from_band_order(xs, batch, seq).transpose(1, 0, 2).reshape(seq * batch, d)
                band = False
            wbu, are, aim, wc = _s5_weights(s5_lam_re[j], s5_lam_im[j], s5_log_step[j],
                                            s5_b_re[j], s5_b_im[j], s5_c_re[j], s5_c_im[j])
            y = _s5_call(xs, g_mix, wbu, are, aim, wc, s5_d[j].reshape(1, d), batch=batch)
            xs = _glu_call(y, xs, s5_w_glu[j].astype(BF16))
        else:
            if not band:
                xs = _to_band_order(xs, batch, seq)
                band = True
            xs = _attention_layer(xs, g_mix, attn_w_qkv[j], attn_w_o[j], batch=batch, seq=seq)
        xs = _peer_layer(xs, norm_ffn[i].reshape(1, d), peer_w_q[i], peer_sub_keys[i],
                         peer_u[i], peer_v[i], g_final, final_norm=(i == depth - 1))
    if band:
        return _from_band_order(xs, batch, seq)
    return xs.reshape(seq, batch, d).transpose(1, 0, 2)
```

```python
import functools
import math

import jax
import jax.numpy as jnp
import numpy as np
from jax import lax
from jax.experimental import pallas as pl
from jax.experimental.pallas import tpu as pltpu

F32 = jnp.float32
BF16 = jnp.bfloat16

RMS_EPS = 1e-6
SSM_GROUP = 16
SSM_STATE = 64
SSM_BLOCK_GROUPS = 16
HEAD_DIM = 64
DILATED_GROUPS = ((128, 1), (512, 4), (2048, 16))
ATTN_BLOCK = 128
ROPE_THETA = 10000.0
PEER_HEADS = 8
PEER_KEYS = 128
PEER_HALF = 128
PEER_TOPK = 16
NEG_BIG = -1e30

LANES = 128
SUBLANES = 8
BF16_ROWS = 16
MXU_WIDTH = 256
VMEM_LIMIT_BYTES = 56 * 1024 * 1024


def _params(*sem):
    return pltpu.CompilerParams(dimension_semantics=sem, vmem_limit_bytes=VMEM_LIMIT_BYTES)


def _rms(x, g):
    return x * lax.rsqrt(jnp.mean(x * x, axis=-1, keepdims=True) + RMS_EPS) * g


def _gelu(x):
    c = math.sqrt(2.0 / math.pi)
    return 0.5 * x * (1.0 + jnp.tanh(c * (x + 0.044715 * (x * x * x))))


def _s5_kernel(x_ref, g_ref, wbu_ref, are_ref, aim_ref, wc_ref, d_ref, y_ref, bu_ref, st_ref,
               *, batch, nblk):
    @pl.when(pl.program_id(0) == 0)
    def _():
        st_ref[...] = jnp.zeros_like(st_ref)

    rows = x_ref.shape[0]
    steps = rows // batch
    h = _rms(x_ref[...], g_ref[...])
    hb = h.astype(BF16)
    kin = wbu_ref.shape[1]
    half = wbu_ref.shape[2] // 2
    for c in range(nblk):
        bu_ref[...] = jnp.dot(hb[:, c * kin:(c + 1) * kin], wbu_ref[c],
                              preferred_element_type=F32)
        are = jnp.broadcast_to(are_ref[c], (batch, half))
        aim = jnp.broadcast_to(aim_ref[c], (batch, half))

        def step(t, carry):
            sre, sim = carry
            r = pl.multiple_of(t * batch, batch)
            bre = bu_ref[pl.ds(r, batch), 0:half]
            bim = bu_ref[pl.ds(r, batch), half:2 * half]
            nre = are * sre - aim * sim + bre
            nim = are * sim + aim * sre + bim
            bu_ref[pl.ds(r, batch), 0:half] = nre
            bu_ref[pl.ds(r, batch), half:2 * half] = nim
            return nre, nim

        sre, sim = lax.fori_loop(0, steps, step,
                                 (st_ref[c, :, 0:half], st_ref[c, :, half:2 * half]),
                                 unroll=True)
        st_ref[c, :, 0:half] = sre
        st_ref[c, :, half:2 * half] = sim
        yc = jnp.dot(bu_ref[...].astype(BF16), wc_ref[c], preferred_element_type=F32)
        yc = yc + d_ref[:, c * kin:(c + 1) * kin] * h[:, c * kin:(c + 1) * kin]
        y_ref[:, c * kin:(c + 1) * kin] = _gelu(yc).astype(BF16)


def _s5_weights(lam_re, lam_im, log_step, b_re, b_im, c_re, c_im):
    G, P = lam_re.shape
    H = b_re.shape[-1]
    step = jnp.exp(log_step)[:, None]
    mag = jnp.exp(lam_re * step)
    lb_re = mag * jnp.cos(lam_im * step)
    lb_im = mag * jnp.sin(lam_im * step)
    den = lam_re * lam_re + lam_im * lam_im
    num_re = lb_re - 1.0
    coef_re = (num_re * lam_re + lb_im * lam_im) / den
    coef_im = (lb_im * lam_re - num_re * lam_im) / den
    bb_re = coef_re[..., None] * b_re - coef_im[..., None] * b_im
    bb_im = coef_re[..., None] * b_im + coef_im[..., None] * b_re
    gb = SSM_BLOCK_GROUPS
    nblk = G // gb
    eye = jnp.eye(gb, dtype=F32)

    def bdiag_in(w):
        w = w.reshape(nblk, gb, P, H)
        return jnp.einsum('cgph,gk->cghkp', w, eye).reshape(nblk, gb * H, gb * P)

    def bdiag_out(w):
        w = w.reshape(nblk, gb, H, P)
        return jnp.einsum('cghp,gk->cgpkh', w, eye).reshape(nblk, gb * P, gb * H)

    wbu = jnp.concatenate([bdiag_in(bb_re), bdiag_in(bb_im)], axis=-1).astype(BF16)
    wc = jnp.concatenate([bdiag_out(c_re), -bdiag_out(c_im)], axis=1).astype(BF16)
    are = lb_re.reshape(nblk, 1, gb * P)
    aim = lb_im.reshape(nblk, 1, gb * P)
    return wbu, are, aim, wc


def _s5_call(x, g, wbu, are, aim, wc, d_skip, *, batch, steps_per_tile=64):
    n, d = x.shape
    nblk = wbu.shape[0]
    rows = batch * steps_per_tile
    assert n % rows == 0
    const3 = lambda i: (0, 0, 0)
    const2 = lambda i: (0, 0)
    return pl.pallas_call(
        functools.partial(_s5_kernel, batch=batch, nblk=nblk),
        grid=(n // rows,),
        in_specs=[pl.BlockSpec((rows, d), lambda i: (i, 0)),
                  pl.BlockSpec((1, d), const2),
                  pl.BlockSpec(wbu.shape, const3),
                  pl.BlockSpec(are.shape, const3),
                  pl.BlockSpec(aim.shape, const3),
                  pl.BlockSpec(wc.shape, const3),
                  pl.BlockSpec((1, d), const2)],
        out_specs=pl.BlockSpec((rows, d), lambda i: (i, 0)),
        out_shape=jax.ShapeDtypeStruct((n, d), BF16),
        scratch_shapes=[pltpu.VMEM((rows, wbu.shape[2]), F32),
                        pltpu.VMEM((nblk, batch, wbu.shape[2]), F32)],
        compiler_params=_params("arbitrary"),
        name="s5_ssm",
    )(x, g, wbu, are, aim, wc, d_skip)


def _glu_kernel(y_ref, x_ref, w_ref, o_ref):
    z = jnp.dot(y_ref[...], w_ref[...], preferred_element_type=F32)
    d = o_ref.shape[1]
    o_ref[...] = x_ref[...] + z[:, :d] * jax.nn.sigmoid(z[:, d:])


def _glu_call(y, x, w, *, tile=512):
    n, d = x.shape
    return pl.pallas_call(
        _glu_kernel,
        grid=(n // tile,),
        in_specs=[pl.BlockSpec((tile, d), lambda i: (i, 0)),
                  pl.BlockSpec((tile, d), lambda i: (i, 0)),
                  pl.BlockSpec(w.shape, lambda i: (0, 0))],
        out_specs=pl.BlockSpec((tile, d), lambda i: (i, 0)),
        out_shape=jax.ShapeDtypeStruct((n, d), F32),
        compiler_params=_params("parallel"),
        name="s5_glu",
    )(y, x, w)


def _rope_perm(n_heads):
    half = HEAD_DIM // 2
    idx = []
    for p in range(n_heads // 2):
        h0, h1 = 2 * p, 2 * p + 1
        for part in (0, 1):
            for h in (h0, h1):
                idx.extend(range(h * HEAD_DIM + part * half, h * HEAD_DIM + (part + 1) * half))
    return np.asarray(idx, dtype=np.int32)


BAND_MACRO = ATTN_BLOCK * 16


def _to_band_order(x, batch, seq):
    d = x.shape[1]
    x = x.reshape(seq // BAND_MACRO, ATTN_BLOCK, 16, batch, d)
    return x.transpose(0, 2, 3, 1, 4).reshape(seq * batch, d)


def _from_band_order(x, batch, seq):
    d = x.shape[1]
    x = x.reshape(seq // BAND_MACRO, 16, batch, ATTN_BLOCK, d)
    return x.transpose(2, 0, 3, 1, 4).reshape(batch, seq, d)


def _qkv_kernel(x_ref, g_ref, w_ref, cos_ref, sin_ref, q_ref, k_ref, v_ref):
    d = x_ref.shape[1]
    hb = _rms(x_ref[...], g_ref[...]).astype(BF16)
    reps = x_ref.shape[0] // cos_ref.shape[0]
    cos = jnp.tile(cos_ref[...], (reps, 1))
    sin = jnp.tile(sin_ref[...], (reps, 1))
    scale = HEAD_DIM ** -0.5

    def roped(col0, ref, mul):
        for c in range(d // MXU_WIDTH):
            both = jnp.dot(hb, w_ref[:, col0 + c * MXU_WIDTH:col0 + (c + 1) * MXU_WIDTH],
                           preferred_element_type=F32)
            for half in range(MXU_WIDTH // LANES):
                blk = both[:, half * LANES:(half + 1) * LANES]
                out = blk * cos + pltpu.roll(blk, HEAD_DIM, axis=1) * sin
                if mul != 1.0:
                    out = out * mul
                lo = c * MXU_WIDTH + half * LANES
                ref[:, lo:lo + LANES] = out.astype(ref.dtype)

    roped(0, q_ref, scale)
    roped(d, k_ref, 1.0)
    v_ref[...] = jnp.dot(hb, w_ref[:, 2 * d:3 * d],
                         preferred_element_type=F32).astype(v_ref.dtype)


def _qkv_call(x, g, w, cos, sin, *, batch, dtype, tile=512):
    n, d = x.shape
    per_table = batch * ATTN_BLOCK
    assert per_table % tile == 0 and tile % ATTN_BLOCK == 0
    row = pl.BlockSpec((tile, d), lambda i: (i, 0))
    tab = pl.BlockSpec((None, ATTN_BLOCK, LANES), lambda i: (i * tile // per_table, 0, 0))
    out = jax.ShapeDtypeStruct((n, d), dtype)
    return pl.pallas_call(
        _qkv_kernel,
        grid=(n // tile,),
        in_specs=[row, pl.BlockSpec((1, d), lambda i: (0, 0)),
                  pl.BlockSpec(w.shape, lambda i: (0, 0)), tab, tab],
        out_specs=[row, row, row],
        out_shape=[out, out, out],
        compiler_params=_params("parallel"),
        name="attn_qkv",
    )(x, g, w, cos, sin)


ATTN_CLASSES_PER_STEP = 8


def _attn_kernel(q_ref, kp_ref, kc_ref, vp_ref, vc_ref, o_ref, l_ref, *, chunk):
    c = ATTN_BLOCK
    d = q_ref.shape[-1]
    first = pl.program_id(1) == 0

    def pos(r):
        return (r % chunk) * (c // chunk) + r // chunk

    qrow = lax.broadcasted_iota(jnp.int32, (c, 2 * c), 0)
    kcol = lax.broadcasted_iota(jnp.int32, (c, 2 * c), 1)
    cur = kcol >= c
    kpos = jnp.where(cur, pos(kcol - c) + c, pos(kcol))
    dist = pos(qrow) + c - kpos
    valid = (dist >= 0) & (dist <= c) & jnp.logical_or(cur, jnp.logical_not(first))
    lane = lax.broadcasted_iota(jnp.int32, (c, LANES), 1)
    qmask0 = (lane // (HEAD_DIM // 2)) % 2 == 0
    omask0 = lane < HEAD_DIM
    blk = (q_ref.shape[0], chunk, LANES)

    def load(ref, cl, sl):
        return ref[:, cl, :, sl].reshape(c, LANES).astype(BF16)

    for cl in range(q_ref.shape[1]):
        lse = jnp.zeros((c, LANES), F32)
        for p in range(d // LANES):
            sl = slice(p * LANES, (p + 1) * LANES)
            qp = load(q_ref, cl, sl)
            kp = jnp.concatenate([load(kp_ref, cl, sl), load(kc_ref, cl, sl)], axis=0)
            vp = jnp.concatenate([load(vp_ref, cl, sl), load(vc_ref, cl, sl)], axis=0)
            outs = []
            for e in range(2):
                qm = jnp.where(qmask0 if e == 0 else jnp.logical_not(qmask0), qp,
                               jnp.zeros_like(qp))
                s = lax.dot_general(qm, kp, (((1,), (1,)), ((), ())),
                                    preferred_element_type=F32)
                s = jnp.where(valid, s, NEG_BIG)
                smax = jnp.max(s, axis=-1, keepdims=True)
                ex = jnp.exp(s - smax)
                den = jnp.sum(ex, axis=-1, keepdims=True)
                o = jnp.dot(ex.astype(BF16), vp, preferred_element_type=F32)
                outs.append(o / den)
                lse = jnp.where(lane == 2 * p + e, smax + jnp.log(den), lse)
            o_ref[:, cl, :, sl] = jnp.where(omask0, outs[0], outs[1]).astype(
                o_ref.dtype).reshape(blk)
        l_ref[:, cl] = lse.reshape(blk)


def _attn_call(q, k, v, *, dilation, batch, seq):
    n, d = q.shape
    c = ATTN_BLOCK
    macros = seq // BAND_MACRO
    per = 16 // dilation
    chunk = c // per
    ncls = ATTN_CLASSES_PER_STEP
    view = (macros, per, dilation * batch, per, chunk, d)
    block = (None, per, ncls, None, chunk, d)

    def at(s, i):
        return (i // per, 0, s, i % per, 0, 0)

    cur = pl.BlockSpec(block, at)
    prev = pl.BlockSpec(block, lambda s, i: at(s, jnp.maximum(i - 1, 0)))
    lse_view = view[:-1] + (LANES,)
    o, lse = pl.pallas_call(
        functools.partial(_attn_kernel, chunk=chunk),
        grid=(dilation * batch // ncls, macros * per),
        in_specs=[cur, prev, cur, prev, cur],
        out_specs=[cur, pl.BlockSpec(block[:-1] + (LANES,), at)],
        out_shape=[jax.ShapeDtypeStruct(view, q.dtype), jax.ShapeDtypeStruct(lse_view, F32)],
        compiler_params=_params("parallel", "parallel"),
        name="attn_band",
    )(q.reshape(view), k.reshape(view), k.reshape(view), v.reshape(view), v.reshape(view))
    return o.reshape(n, d), lse.reshape(n, LANES)


def _attn_out_kernel(x_ref, o0, o1, o2, l0, l1, l2, e_ref, w_ref, y_ref):
    a, b, c = l0[...], l1[...], l2[...]
    mx = jnp.maximum(jnp.maximum(a, b), c)
    ea, eb, ec = jnp.exp(a - mx), jnp.exp(b - mx), jnp.exp(c - mx)
    inv = 1.0 / (ea + eb + ec)

    def spread(wt):
        hi = wt.astype(BF16)
        lo = (wt - hi.astype(F32)).astype(BF16)
        return (jnp.dot(hi, e_ref[...], preferred_element_type=F32)
                + jnp.dot(lo, e_ref[...], preferred_element_type=F32))

    o = spread(ea * inv) * o0[...].astype(F32) + spread(eb * inv) * o1[...].astype(F32) \
        + spread(ec * inv) * o2[...].astype(F32)
    y_ref[...] = x_ref[...] + jnp.dot(o.astype(BF16), w_ref[...], preferred_element_type=F32)


def _attn_out_call(x, os_, ls_, w, *, tile=512):
    n, d = x.shape
    row = pl.BlockSpec((tile, d), lambda i: (i, 0))
    lrow = pl.BlockSpec((tile, LANES), lambda i: (i, 0))
    heads = jnp.arange(LANES, dtype=jnp.int32)[:, None]
    spread = (jnp.arange(d, dtype=jnp.int32)[None, :] // HEAD_DIM == heads).astype(BF16)
    return pl.pallas_call(
        _attn_out_kernel,
        grid=(n // tile,),
        in_specs=[row] * 4 + [lrow] * 3 + [pl.BlockSpec(spread.shape, lambda i: (0, 0)),
                                           pl.BlockSpec(w.shape, lambda i: (0, 0))],
        out_specs=row,
        out_shape=jax.ShapeDtypeStruct((n, d), F32),
        compiler_params=_params("parallel"),
        name="attn_out",
    )(x, *os_, *ls_, spread, w)


def _topk_rank(s):
    nk, t = s.shape
    iota = lax.broadcasted_iota(jnp.int32, (nk, t), 0)
    row16 = lax.broadcasted_iota(jnp.int32, (PEER_TOPK, t), 0)
    rank = jnp.full((nk, t), 99, jnp.int32)
    vals = jnp.zeros((PEER_TOPK, t), F32)
    for k in range(PEER_TOPK):
        m = jnp.max(s, axis=0, keepdims=True)
        idx = jnp.min(jnp.where(s == m, iota, nk), axis=0, keepdims=True)
        sel = iota == idx
        rank = jnp.where(sel, k, rank)
        s = jnp.where(sel, -jnp.inf, s)
        vals = jnp.where(row16 == k, m, vals)
    return rank, vals


SENTINEL_BASE = 1e38
SENTINEL_STEP = 1e37
SENTINEL_LIMIT = -0.95e38


def _topk_rank_distinct(s):
    nk, t = s.shape
    row16 = lax.broadcasted_iota(jnp.int32, (PEER_TOPK, t), 0)
    low = jnp.min(s, axis=0, keepdims=True)
    vals = jnp.zeros((PEER_TOPK, t), F32)
    for k in range(PEER_TOPK):
        m = jnp.max(s, axis=0, keepdims=True)
        s = jnp.where(s == m, -(SENTINEL_BASE + k * SENTINEL_STEP), s)
        vals = jnp.where(row16 == k, m, vals)
    top = s < SENTINEL_LIMIT
    rank = jnp.where(top, jnp.floor(s * (-1.0 / SENTINEL_STEP) - (SENTINEL_BASE / SENTINEL_STEP - 0.5)),
                     99.0)
    cnt = jnp.sum(top.astype(F32), axis=0, keepdims=True)
    bad = jnp.logical_or(cnt != float(PEER_TOPK), jnp.logical_not(low > SENTINEL_LIMIT))
    return rank, vals, bad.astype(jnp.int32)


_CAND_BLOCKS = ((0, 0, 8), (0, 8, 8), (1, 0, 8), (2, 0, 5), (3, 0, 4), (4, 0, 3), (5, 0, 2),
                (6, 0, 2), (7, 0, 2))


def _candidates(a, b):
    t = a.shape[1]
    sub = lax.broadcasted_iota(jnp.int32, (8, t), 0)
    cands, flats, valid = [], [], []
    for k, l0, cnt in _CAND_BLOCKS:
        cnd = a[k:k + 1, :] + b[l0:l0 + 8, :]
        if cnt < 8:
            cnd = jnp.where(sub < cnt, cnd, -jnp.inf)
        cands.append(cnd)
        flats.append(sub + (k * PEER_TOPK + l0))
        valid.append(None if cnt == 8 else sub < cnt)
    cands.append(a[8:16, :] + b[0:1, :])
    flats.append((sub + 8) * PEER_TOPK)
    valid.append(None)
    return cands, flats, valid


def _counts_from_sels(sels):
    t = sels[0].shape[1]
    row16 = lax.broadcasted_iota(jnp.int32, (PEER_TOPK, t), 0)
    selfs = [s_.astype(F32) for s_ in sels]
    per_k = [jnp.sum(selfs[0] + selfs[1], axis=0, keepdims=True)]
    per_k += [jnp.sum(s_, axis=0, keepdims=True) for s_ in selfs[2:9]]
    counts = jnp.concatenate([jnp.zeros((8, t), F32), selfs[9]], axis=0)
    for k in range(8):
        counts = jnp.where(row16 == k, per_k[k], counts)
    return counts


def _pair_counts(a, b):
    t = a.shape[1]
    cands, flats, _ = _candidates(a, b)
    top = a[0:1, :] + b[0:1, :]
    sels = [jnp.zeros((8, t), jnp.bool_) for _ in cands]
    zsum = jnp.zeros((1, t), F32)
    big = PEER_TOPK * PEER_TOPK
    for _ in range(PEER_TOPK):
        m = functools.reduce(jnp.maximum, cands)
        m = jnp.max(m, axis=0, keepdims=True)
        idx = functools.reduce(jnp.minimum,
                               [jnp.where(c == m, f, big) for c, f in zip(cands, flats)])
        idx = jnp.min(idx, axis=0, keepdims=True)
        hit = [f == idx for f in flats]
        cands = [jnp.where(h_, -jnp.inf, c) for h_, c in zip(hit, cands)]
        sels = [jnp.logical_or(s_, h_) for s_, h_ in zip(sels, hit)]
        zsum = zsum + jnp.exp(m - top)
    return _counts_from_sels(sels), zsum


def _pair_counts_distinct(a, b):
    cands, _, valid = _candidates(a, b)
    top = a[0:1, :] + b[0:1, :]
    zsum = jnp.zeros((1, a.shape[1]), F32)
    for _ in range(PEER_TOPK):
        m = functools.reduce(jnp.maximum, cands)
        m = jnp.max(m, axis=0, keepdims=True)
        cands = [jnp.where(c == m, -jnp.inf, c) for c in cands]
        zsum = zsum + jnp.exp(m - top)
    sels = [c == -jnp.inf if v is None else jnp.logical_and(c == -jnp.inf, v)
            for c, v in zip(cands, valid)]
    counts = _counts_from_sels(sels)
    total = jnp.sum(counts, axis=0, keepdims=True)
    return counts, zsum, (total != float(PEER_TOPK)).astype(jnp.int32)


def _route_head(s1, s2, exact):
    if exact:
        rank1, a = _topk_rank(s1)
        rank2, b = _topk_rank(s2)
        counts, zsum = _pair_counts(a, b)
        bad = None
    else:
        rank1, a, bad1 = _topk_rank_distinct(s1)
        rank2, b, bad2 = _topk_rank_distinct(s2)
        counts, zsum, bad3 = _pair_counts_distinct(a, b)
        bad = bad1 + bad2 + bad3
    idx = rank1.astype(jnp.int32)
    sub = idx & (SUBLANES - 1)
    lo = jnp.take_along_axis(counts[:SUBLANES], sub, axis=0)
    hi = jnp.take_along_axis(counts[SUBLANES:], sub, axis=0)
    nn = jnp.where(idx < SUBLANES, lo, jnp.where(idx < PEER_TOPK, hi, 0.0))
    r2 = rank2.astype(F32).astype(BF16)
    e2 = jnp.exp(s2 - b[0:1, :]).astype(BF16)
    p = jnp.exp(s1 - a[0:1, :]) / zsum
    return (r2, e2, nn, p), bad


def _route_kernel(x_ref, g_ref, wq_ref, sk_ref, hb_ref, r2_ref, e2_ref, nn_ref, p_ref, sc_ref):
    hb = _rms(x_ref[...], g_ref[...]).astype(BF16)
    hb_ref[...] = hb
    qt = lax.dot_general(wq_ref[...], hb, (((1,), (1,)), ((), ())),
                         preferred_element_type=F32).astype(BF16)
    for hp in range(2 * PEER_HEADS):
        sc_ref[hp] = jnp.dot(sk_ref[hp], qt[hp * PEER_HALF:(hp + 1) * PEER_HALF],
                             preferred_element_type=F32)
    slabs = nn_ref.shape[1]

    def head(h, carry):
        s1 = sc_ref[2 * h]
        s2 = sc_ref[2 * h + 1]

        def store(vals):
            r2, e2, nn, p = vals
            r2_ref[h] = r2
            e2_ref[h] = e2
            for c in range(slabs):
                nn_ref[h, c] = nn[:, c * LANES:(c + 1) * LANES]
                p_ref[h, c] = p[:, c * LANES:(c + 1) * LANES]

        vals, bad = _route_head(s1, s2, exact=False)
        store(vals)

        @pl.when(jnp.max(bad) > 0)
        def _():
            store(_route_head(s1, s2, exact=True)[0])

        return carry

    lax.fori_loop(0, PEER_HEADS, head, 0)


def _route_call(x, g, wq_t, sk, *, tile=512):
    n, d = x.shape
    hk = (PEER_HEADS, PEER_KEYS, n)
    hs = (PEER_HEADS, n // LANES, PEER_KEYS, LANES)
    blk = pl.BlockSpec((PEER_HEADS, PEER_KEYS, tile), lambda i: (0, 0, i))
    slab = pl.BlockSpec((PEER_HEADS, tile // LANES, PEER_KEYS, LANES), lambda i: (0, i, 0, 0))
    return pl.pallas_call(
        _route_kernel,
        grid=(n // tile,),
        in_specs=[pl.BlockSpec((tile, d), lambda i: (i, 0)),
                  pl.BlockSpec((1, d), lambda i: (0, 0)),
                  pl.BlockSpec(wq_t.shape, lambda i: (0, 0)),
                  pl.BlockSpec(sk.shape, lambda i: (0, 0, 0))],
        out_specs=[pl.BlockSpec((tile, d), lambda i: (i, 0)), blk, blk, slab, slab],
        out_shape=[jax.ShapeDtypeStruct((n, d), BF16),
                   jax.ShapeDtypeStruct(hk, BF16), jax.ShapeDtypeStruct(hk, BF16),
                   jax.ShapeDtypeStruct(hs, F32), jax.ShapeDtypeStruct(hs, F32)],
        scratch_shapes=[pltpu.VMEM((2 * PEER_HEADS, PEER_KEYS, tile), F32)],
        compiler_params=_params("parallel"),
        name="peer_route",
    )(x, g, wq_t, sk)


def _gelu_sig(x):
    k0 = -2.0 * math.sqrt(2.0 / math.pi) * math.log2(math.e)
    return x / (1.0 + jnp.exp2(x * (k0 + (k0 * 0.044715) * (x * x))))


def _row_bf16(ref, h, i, rows):
    parts = [jnp.broadcast_to(ref[h, c, i:i + 1, :], (BF16_ROWS, LANES)).astype(BF16)
             for c in range(ref.shape[1])]
    row = jnp.concatenate(parts, axis=1)
    return jnp.broadcast_to(row[None], (rows // BF16_ROWS, BF16_ROWS, row.shape[1])).reshape(
        rows, row.shape[1])


def _peer_kernel(x_ref, hb_ref, u_ref, vt_ref, r2_ref, e2_ref, nn_ref, p_ref, gf_ref, o_ref,
                 act_ref, w_ref, acc_ref, *, final_norm, act_splits):
    et = pl.program_id(1)
    te = u_ref.shape[0]
    tm = hb_ref.shape[0]
    rows_i = te // PEER_KEYS

    @pl.when(et == 0)
    def _():
        acc_ref[...] = jnp.zeros_like(acc_ref)

    def gating(ii):
        rs = slice(ii * PEER_KEYS, (ii + 1) * PEER_KEYS)
        gate = jnp.zeros((PEER_KEYS, tm), BF16)
        for h in range(PEER_HEADS):
            nn = _row_bf16(nn_ref, h, ii, PEER_KEYS)
            p = _row_bf16(p_ref, h, ii, PEER_KEYS)
            gate = gate + e2_ref[h] * jnp.where(r2_ref[h] < nn, p, jnp.zeros_like(p))
        w_ref[rs, :] = gate * _gelu_sig(act_ref[rs, :].astype(BF16))

    mrows = te // act_splits
    for m in range(act_splits):
        rs = slice(m * mrows, (m + 1) * mrows)
        act_ref[rs, :] = lax.dot_general(u_ref[rs, :], hb_ref[...], (((1,), (1,)), ((), ())),
                                         preferred_element_type=F32)
        for ii in range(m * rows_i // act_splits, (m + 1) * rows_i // act_splits):
            gating(ii)
    acc_ref[...] += jnp.dot(vt_ref[...], w_ref[...], preferred_element_type=F32)

    @pl.when(et == pl.num_programs(1) - 1)
    def _():
        y = x_ref[...] + acc_ref[...].T
        if final_norm:
            y = _rms(y, gf_ref[...])
        o_ref[...] = y


def _peer_call(x, hb, u, vt, r2, e2, nn, p, g_final, *, final_norm, tm=512, te=2048,
               act_splits=4):
    n, d = x.shape
    rows_i = te // PEER_KEYS
    tok = pl.BlockSpec((tm, d), lambda t, e: (t, 0))
    allj = pl.BlockSpec((PEER_HEADS, PEER_KEYS, tm), lambda t, e: (0, 0, t))
    rowi = pl.BlockSpec((PEER_HEADS, tm // LANES, rows_i, LANES), lambda t, e: (0, t, e, 0))
    return pl.pallas_call(
        functools.partial(_peer_kernel, final_norm=final_norm, act_splits=act_splits),
        grid=(n // tm, u.shape[0] // te),
        in_specs=[tok, tok,
                  pl.BlockSpec((te, d), lambda t, e: (e, 0)),
                  pl.BlockSpec((d, te), lambda t, e: (0, e)),
                  allj, allj, rowi, rowi,
                  pl.BlockSpec((1, d), lambda t, e: (0, 0))],
        out_specs=tok,
        out_shape=jax.ShapeDtypeStruct((n, d), F32),
        scratch_shapes=[pltpu.VMEM((te, tm), F32), pltpu.VMEM((te, tm), BF16),
                        pltpu.VMEM((d, tm), F32)],
        compiler_params=_params("parallel", "arbitrary"),
        name="peer_dense",
    )(x, hb, u, vt, r2, e2, nn, p, g_final)


def _peer_layer(x, g, w_q, sub_keys, u_tab, v_tab, g_final, final_norm):
    wq_t = w_q.T.astype(BF16)
    sk = sub_keys.reshape(PEER_HEADS * 2, PEER_KEYS, PEER_HALF).astype(BF16)
    hb, r2, e2, nn, p = _route_call(x, g, wq_t, sk)
    return _peer_call(x, hb, u_tab.astype(BF16), v_tab.T.astype(BF16), r2, e2, nn, p, g_final,
                      final_norm=final_norm)


def _rope_tables(seq):
    half = HEAD_DIM // 2
    inv = ROPE_THETA ** (-jnp.arange(half, dtype=F32) / half)
    t = jnp.arange(seq, dtype=F32).reshape(seq // BAND_MACRO, ATTN_BLOCK, 16)
    t = t.transpose(0, 2, 1).reshape(-1, ATTN_BLOCK)
    ang = t[:, :, None] * inv
    cos = jnp.tile(jnp.cos(ang), (1, 1, 4))
    sin = jnp.sin(ang)
    return cos, jnp.concatenate([-sin, -sin, sin, sin], axis=2)


def _attention_layer(x, g, w_qkv, w_o, *, batch, seq):
    n, d = x.shape
    perm = _rope_perm(d // HEAD_DIM)
    cos, sin = _rope_tables(seq)
    wg = w_qkv.reshape(d, len(DILATED_GROUPS), 3, d)
    outs, lses = [], []
    for gi, (window, dilation) in enumerate(DILATED_GROUPS):
        assert window // dilation == ATTN_BLOCK and 16 % dilation == 0
        w = jnp.concatenate([wg[:, gi, 0][:, perm], wg[:, gi, 1][:, perm], wg[:, gi, 2]],
                            axis=1).astype(BF16)
        dtype = BF16 if (ATTN_BLOCK * dilation // 16) % 16 == 0 else F32
        q, k, v = _qkv_call(x, g, w, cos, sin, batch=batch, dtype=dtype)
        o, lse = _attn_call(q, k, v, dilation=dilation, batch=batch, seq=seq)
        outs.append(o)
        lses.append(lse)
    return _attn_out_call(x, outs, lses, w_o.astype(BF16))


def kernel(x, norm_mix, norm_ffn, norm_final, s5_lam_re, s5_lam_im, s5_log_step, s5_b_re, s5_b_im, s5_c_re, s5_c_im, s5_d, s5_w_glu, attn_w_qkv, attn_w_o, peer_w_q, peer_sub_keys, peer_u, peer_v):
    batch, seq, d = x.shape
    assert batch == 8, "one timestep of all batches must fill one 8-sublane group"
    assert seq % BAND_MACRO == 0
    depth = norm_mix.shape[0]
    xs = x.transpose(1, 0, 2).reshape(seq * batch, d)
    band = False
    g_final = norm_final.reshape(1, d)
    for i in range(depth):
        j = i // 2
        g_mix = norm_mix[i].reshape(1, d)
        if i % 2 == 0:
            if band:
                xs = _from_band_order(xs, batch, seq).transpose(1, 0, 2).reshape(seq * batch, d)
                band = False
            wbu, are, aim, wc = _s5_weights(s5_lam_re[j], s5_lam_im[j], s5_log_step[j],
                                            s5_b_re[j], s5_b_im[j], s5_c_re[j], s5_c_im[j])
            y = _s5_call(xs, g_mix, wbu, are, aim, wc, s5_d[j].reshape(1, d), batch=batch)
            xs = _glu_call(y, xs, s5_w_glu[j].astype(BF16))
        else:
            if not band:
                xs = _to_band_order(xs, batch, seq)
                band = True
            xs = _attention_layer(xs, g_mix, attn_w_qkv[j], attn_w_o[j], batch=batch, seq=seq)
        xs = _peer_layer(xs, norm_ffn[i].reshape(1, d), peer_w_q[i], peer_sub_keys[i],
                         peer_u[i], peer_v[i], g_final, final_norm=(i == depth - 1))
    if band:
        return _from_band_order(xs, batch, seq)
    return xs.reshape(seq, batch, d).transpose(1, 0, 2)
```

```python
import functools
import math

import jax
import jax.numpy as jnp
import numpy as np
from jax import lax
from jax.experimental import pallas as pl
from jax.experimental.pallas import tpu as pltpu

F32 = jnp.float32
BF16 = jnp.bfloat16

RMS_EPS = 1e-6
SSM_GROUP = 16
SSM_STATE = 64
SSM_BLOCK_GROUPS = 16
HEAD_DIM = 64
DILATED_GROUPS = ((128, 1), (512, 4), (2048, 16))
ATTN_BLOCK = 128
ROPE_THETA = 10000.0
PEER_HEADS = 8
PEER_KEYS = 128
PEER_HALF = 128
PEER_TOPK = 16
NEG_BIG = -1e30

LANES = 128
SUBLANES = 8
BF16_ROWS = 16
MXU_WIDTH = 256
VMEM_LIMIT_BYTES = 56 * 1024 * 1024


def _params(*sem):
    return pltpu.CompilerParams(dimension_semantics=sem, vmem_limit_bytes=VMEM_LIMIT_BYTES)


def _rms(x, g):
    return x * lax.rsqrt(jnp.mean(x * x, axis=-1, keepdims=True) + RMS_EPS) * g


def _gelu(x):
    c = math.sqrt(2.0 / math.pi)
    return 0.5 * x * (1.0 + jnp.tanh(c * (x + 0.044715 * (x * x * x))))


def _s5_kernel(x_ref, g_ref, wbu_ref, are_ref, aim_ref, wc_ref, d_ref, wg_ref, o_ref,
               bu_ref, st_ref, y_ref, *, batch, nblk):
    @pl.when(pl.program_id(0) == 0)
    def _():
        st_ref[...] = jnp.zeros_like(st_ref)

    rows = x_ref.shape[0]
    steps = rows // batch
    h = _rms(x_ref[...], g_ref[...])
    hb = h.astype(BF16)
    kin = wbu_ref.shape[1]
    half = wbu_ref.shape[2] // 2
    for c in range(nblk):
        bu_ref[...] = jnp.dot(hb[:, c * kin:(c + 1) * kin], wbu_ref[c],
                              preferred_element_type=F32)
        are = jnp.broadcast_to(are_ref[c], (batch, half))
        aim = jnp.broadcast_to(aim_ref[c], (batch, half))

        def step(t, carry):
            sre, sim = carry
            r = pl.multiple_of(t * batch, batch)
            bre = bu_ref[pl.ds(r, batch), 0:half]
            bim = bu_ref[pl.ds(r, batch), half:2 * half]
            nre = are * sre - aim * sim + bre
            nim = are * sim + aim * sre + bim
            bu_ref[pl.ds(r, batch), 0:half] = nre
            bu_ref[pl.ds(r, batch), half:2 * half] = nim
            return nre, nim

        sre, sim = lax.fori_loop(0, steps, step,
                                 (st_ref[c, :, 0:half], st_ref[c, :, half:2 * half]),
                                 unroll=True)
        st_ref[c, :, 0:half] = sre
        st_ref[c, :, half:2 * half] = sim
        yc = jnp.dot(bu_ref[...].astype(BF16), wc_ref[c], preferred_element_type=F32)
        yc = yc + d_ref[:, c * kin:(c + 1) * kin] * h[:, c * kin:(c + 1) * kin]
        y_ref[:, c * kin:(c + 1) * kin] = _gelu(yc).astype(BF16)
    z = jnp.dot(y_ref[...], wg_ref[...], preferred_element_type=F32)
    d = o_ref.shape[1]
    o_ref[...] = x_ref[...] + z[:, :d] * jax.nn.sigmoid(z[:, d:])


def _s5_weights(lam_re, lam_im, log_step, b_re, b_im, c_re, c_im):
    G, P = lam_re.shape
    H = b_re.shape[-1]
    step = jnp.exp(log_step)[:, None]
    mag = jnp.exp(lam_re * step)
    lb_re = mag * jnp.cos(lam_im * step)
    lb_im = mag * jnp.sin(lam_im * step)
    den = lam_re * lam_re + lam_im * lam_im
    num_re = lb_re - 1.0
    coef_re = (num_re * lam_re + lb_im * lam_im) / den
    coef_im = (lb_im * lam_re - num_re * lam_im) / den
    bb_re = coef_re[..., None] * b_re - coef_im[..., None] * b_im
    bb_im = coef_re[..., None] * b_im + coef_im[..., None] * b_re
    gb = SSM_BLOCK_GROUPS
    nblk = G // gb
    eye = jnp.eye(gb, dtype=F32)

    def bdiag_in(w):
        w = w.reshape(nblk, gb, P, H)
        return jnp.einsum('cgph,gk->cghkp', w, eye).reshape(nblk, gb * H, gb * P)

    def bdiag_out(w):
        w = w.reshape(nblk, gb, H, P)
        return jnp.einsum('cghp,gk->cgpkh', w, eye).reshape(nblk, gb * P, gb * H)

    wbu = jnp.concatenate([bdiag_in(bb_re), bdiag_in(bb_im)], axis=-1).astype(BF16)
    wc = jnp.concatenate([bdiag_out(c_re), -bdiag_out(c_im)], axis=1).astype(BF16)
    are = lb_re.reshape(nblk, 1, gb * P)
    aim = lb_im.reshape(nblk, 1, gb * P)
    return wbu, are, aim, wc


def _s5_call(x, g, wbu, are, aim, wc, d_skip, w_glu, *, batch, steps_per_tile=64):
    n, d = x.shape
    nblk = wbu.shape[0]
    rows = batch * steps_per_tile
    assert n % rows == 0
    const3 = lambda i: (0, 0, 0)
    const2 = lambda i: (0, 0)
    return pl.pallas_call(
        functools.partial(_s5_kernel, batch=batch, nblk=nblk),
        grid=(n // rows,),
        in_specs=[pl.BlockSpec((rows, d), lambda i: (i, 0)),
                  pl.BlockSpec((1, d), const2),
                  pl.BlockSpec(wbu.shape, const3),
                  pl.BlockSpec(are.shape, const3),
                  pl.BlockSpec(aim.shape, const3),
                  pl.BlockSpec(wc.shape, const3),
                  pl.BlockSpec((1, d), const2),
                  pl.BlockSpec(w_glu.shape, const2)],
        out_specs=pl.BlockSpec((rows, d), lambda i: (i, 0)),
        out_shape=jax.ShapeDtypeStruct((n, d), F32),
        scratch_shapes=[pltpu.VMEM((rows, wbu.shape[2]), F32),
                        pltpu.VMEM((nblk, batch, wbu.shape[2]), F32),
                        pltpu.VMEM((rows, d), BF16)],
        compiler_params=_params("arbitrary"),
        name="s5_mixer",
    )(x, g, wbu, are, aim, wc, d_skip, w_glu)


def _rope_perm(n_heads):
    half = HEAD_DIM // 2
    idx = []
    for p in range(n_heads // 2):
        h0, h1 = 2 * p, 2 * p + 1
        for part in (0, 1):
            for h in (h0, h1):
                idx.extend(range(h * HEAD_DIM + part * half, h * HEAD_DIM + (part + 1) * half))
    return np.asarray(idx, dtype=np.int32)


BAND_MACRO = ATTN_BLOCK * 16


def _to_band_order(x, batch, seq):
    d = x.shape[1]
    x = x.reshape(seq // BAND_MACRO, ATTN_BLOCK, 16, batch, d)
    return x.transpose(0, 2, 3, 1, 4).reshape(seq * batch, d)


def _from_band_order(x, batch, seq):
    d = x.shape[1]
    x = x.reshape(seq // BAND_MACRO, 16, batch, ATTN_BLOCK, d)
    return x.transpose(2, 0, 3, 1, 4).reshape(batch, seq, d)


def _qkv_kernel(x_ref, g_ref, w_ref, cos_ref, sin_ref, q_ref, k_ref, v_ref):
    d = x_ref.shape[1]
    hb = _rms(x_ref[...], g_ref[...]).astype(BF16)
    reps = x_ref.shape[0] // cos_ref.shape[0]
    cos = jnp.tile(cos_ref[...], (reps, 1))
    sin = jnp.tile(sin_ref[...], (reps, 1))
    scale = HEAD_DIM ** -0.5

    def roped(col0, ref, mul):
        for c in range(d // MXU_WIDTH):
            both = jnp.dot(hb, w_ref[:, col0 + c * MXU_WIDTH:col0 + (c + 1) * MXU_WIDTH],
                           preferred_element_type=F32)
            for half in range(MXU_WIDTH // LANES):
                blk = both[:, half * LANES:(half + 1) * LANES]
                out = blk * cos + pltpu.roll(blk, HEAD_DIM, axis=1) * sin
                if mul != 1.0:
                    out = out * mul
                lo = c * MXU_WIDTH + half * LANES
                ref[:, lo:lo + LANES] = out.astype(ref.dtype)

    roped(0, q_ref, scale)
    roped(d, k_ref, 1.0)
    v_ref[...] = jnp.dot(hb, w_ref[:, 2 * d:3 * d],
                         preferred_element_type=F32).astype(v_ref.dtype)


def _qkv_call(x, g, w, cos, sin, *, batch, dtype, tile=512):
    n, d = x.shape
    per_table = batch * ATTN_BLOCK
    assert per_table % tile == 0 and tile % ATTN_BLOCK == 0
    row = pl.BlockSpec((tile, d), lambda i: (i, 0))
    tab = pl.BlockSpec((None, ATTN_BLOCK, LANES), lambda i: (i * tile // per_table, 0, 0))
    out = jax.ShapeDtypeStruct((n, d), dtype)
    return pl.pallas_call(
        _qkv_kernel,
        grid=(n // tile,),
        in_specs=[row, pl.BlockSpec((1, d), lambda i: (0, 0)),
                  pl.BlockSpec(w.shape, lambda i: (0, 0)), tab, tab],
        out_specs=[row, row, row],
        out_shape=[out, out, out],
        compiler_params=_params("parallel"),
        name="attn_qkv",
    )(x, g, w, cos, sin)


ATTN_CLASSES_PER_STEP = 8


def _attn_kernel(q_ref, kp_ref, kc_ref, vp_ref, vc_ref, o_ref, l_ref, *, chunk):
    c = ATTN_BLOCK
    d = q_ref.shape[-1]
    first = pl.program_id(1) == 0

    def pos(r):
        return (r % chunk) * (c // chunk) + r // chunk

    qrow = lax.broadcasted_iota(jnp.int32, (c, 2 * c), 0)
    kcol = lax.broadcasted_iota(jnp.int32, (c, 2 * c), 1)
    cur = kcol >= c
    kpos = jnp.where(cur, pos(kcol - c) + c, pos(kcol))
    dist = pos(qrow) + c - kpos
    valid = (dist >= 0) & (dist <= c) & jnp.logical_or(cur, jnp.logical_not(first))
    lane = lax.broadcasted_iota(jnp.int32, (c, LANES), 1)
    qmask0 = (lane // (HEAD_DIM // 2)) % 2 == 0
    omask0 = lane < HEAD_DIM
    blk = (q_ref.shape[0], chunk, LANES)

    def load(ref, cl, sl):
        return ref[:, cl, :, sl].reshape(c, LANES).astype(BF16)

    for cl in range(q_ref.shape[1]):
        lse = jnp.zeros((c, LANES), F32)
        for p in range(d // LANES):
            sl = slice(p * LANES, (p + 1) * LANES)
            qp = load(q_ref, cl, sl)
            kp = jnp.concatenate([load(kp_ref, cl, sl), load(kc_ref, cl, sl)], axis=0)
            vp = jnp.concatenate([load(vp_ref, cl, sl), load(vc_ref, cl, sl)], axis=0)
            outs = []
            for e in range(2):
                qm = jnp.where(qmask0 if e == 0 else jnp.logical_not(qmask0), qp,
                               jnp.zeros_like(qp))
                s = lax.dot_general(qm, kp, (((1,), (1,)), ((), ())),
                                    preferred_element_type=F32)
                s = jnp.where(valid, s, NEG_BIG)
                smax = jnp.max(s, axis=-1, keepdims=True)
                ex = jnp.exp(s - smax)
                den = jnp.sum(ex, axis=-1, keepdims=True)
                o = jnp.dot(ex.astype(BF16), vp, preferred_element_type=F32)
                outs.append(o / den)
                lse = jnp.where(lane == 2 * p + e, smax + jnp.log(den), lse)
            o_ref[:, cl, :, sl] = jnp.where(omask0, outs[0], outs[1]).astype(
                o_ref.dtype).reshape(blk)
        l_ref[:, cl] = lse.reshape(blk)


def _attn_call(q, k, v, *, dilation, batch, seq):
    n, d = q.shape
    c = ATTN_BLOCK
    macros = seq // BAND_MACRO
    per = 16 // dilation
    chunk = c // per
    ncls = ATTN_CLASSES_PER_STEP
    view = (macros, per, dilation * batch, per, chunk, d)
    block = (None, per, ncls, None, chunk, d)

    def at(s, i):
        return (i // per, 0, s, i % per, 0, 0)

    cur = pl.BlockSpec(block, at)
    prev = pl.BlockSpec(block, lambda s, i: at(s, jnp.maximum(i - 1, 0)))
    lse_view = view[:-1] + (LANES,)
    o, lse = pl.pallas_call(
        functools.partial(_attn_kernel, chunk=chunk),
        grid=(dilation * batch // ncls, macros * per),
        in_specs=[cur, prev, cur, prev, cur],
        out_specs=[cur, pl.BlockSpec(block[:-1] + (LANES,), at)],
        out_shape=[jax.ShapeDtypeStruct(view, q.dtype), jax.ShapeDtypeStruct(lse_view, F32)],
        compiler_params=_params("parallel", "parallel"),
        name="attn_band",
    )(q.reshape(view), k.reshape(view), k.reshape(view), v.reshape(view), v.reshape(view))
    return o.reshape(n, d), lse.reshape(n, LANES)


def _attn_out_kernel(x_ref, o0, o1, o2, l0, l1, l2, e_ref, w_ref, y_ref):
    a, b, c = l0[...], l1[...], l2[...]
    mx = jnp.maximum(jnp.maximum(a, b), c)
    ea, eb, ec = jnp.exp(a - mx), jnp.exp(b - mx), jnp.exp(c - mx)
    inv = 1.0 / (ea + eb + ec)

    def spread(wt):
        hi = wt.astype(BF16)
        lo = (wt - hi.astype(F32)).astype(BF16)
        return (jnp.dot(hi, e_ref[...], preferred_element_type=F32)
                + jnp.dot(lo, e_ref[...], preferred_element_type=F32))

    o = spread(ea * inv) * o0[...].astype(F32) + spread(eb * inv) * o1[...].astype(F32) \
        + spread(ec * inv) * o2[...].astype(F32)
    y_ref[...] = x_ref[...] + jnp.dot(o.astype(BF16), w_ref[...], preferred_element_type=F32)


def _attn_out_call(x, os_, ls_, w, *, tile=512):
    n, d = x.shape
    row = pl.BlockSpec((tile, d), lambda i: (i, 0))
    lrow = pl.BlockSpec((tile, LANES), lambda i: (i, 0))
    heads = jnp.arange(LANES, dtype=jnp.int32)[:, None]
    spread = (jnp.arange(d, dtype=jnp.int32)[None, :] // HEAD_DIM == heads).astype(BF16)
    return pl.pallas_call(
        _attn_out_kernel,
        grid=(n // tile,),
        in_specs=[row] * 4 + [lrow] * 3 + [pl.BlockSpec(spread.shape, lambda i: (0, 0)),
                                           pl.BlockSpec(w.shape, lambda i: (0, 0))],
        out_specs=row,
        out_shape=jax.ShapeDtypeStruct((n, d), F32),
        compiler_params=_params("parallel"),
        name="attn_out",
    )(x, *os_, *ls_, spread, w)


def _topk_rank(s):
    nk, t = s.shape
    iota = lax.broadcasted_iota(jnp.int32, (nk, t), 0)
    row16 = lax.broadcasted_iota(jnp.int32, (PEER_TOPK, t), 0)
    rank = jnp.full((nk, t), 99, jnp.int32)
    vals = jnp.zeros((PEER_TOPK, t), F32)
    for k in range(PEER_TOPK):
        m = jnp.max(s, axis=0, keepdims=True)
        idx = jnp.min(jnp.where(s == m, iota, nk), axis=0, keepdims=True)
        sel = iota == idx
        rank = jnp.where(sel, k, rank)
        s = jnp.where(sel, -jnp.inf, s)
        vals = jnp.where(row16 == k, m, vals)
    return rank, vals


SENTINEL_BASE = 1e38
SENTINEL_STEP = 1e37
SENTINEL_LIMIT = -0.95e38


def _topk_rank_distinct(s):
    nk, t = s.shape
    row16 = lax.broadcasted_iota(jnp.int32, (PEER_TOPK, t), 0)
    low = jnp.min(s, axis=0, keepdims=True)
    vals = jnp.zeros((PEER_TOPK, t), F32)
    for k in range(PEER_TOPK):
        m = jnp.max(s, axis=0, keepdims=True)
        s = jnp.where(s == m, -(SENTINEL_BASE + k * SENTINEL_STEP), s)
        vals = jnp.where(row16 == k, m, vals)
    top = s < SENTINEL_LIMIT
    rank = jnp.where(top, jnp.floor(s * (-1.0 / SENTINEL_STEP) - (SENTINEL_BASE / SENTINEL_STEP - 0.5)),
                     99.0)
    cnt = jnp.sum(top.astype(F32), axis=0, keepdims=True)
    bad = jnp.logical_or(cnt != float(PEER_TOPK), jnp.logical_not(low > SENTINEL_LIMIT))
    return rank, vals, bad.astype(jnp.int32)


_CAND_BLOCKS = ((0, 0, 8), (0, 8, 8), (1, 0, 8), (2, 0, 5), (3, 0, 4), (4, 0, 3), (5, 0, 2),
                (6, 0, 2), (7, 0, 2))


def _candidates(a, b):
    t = a.shape[1]
    sub = lax.broadcasted_iota(jnp.int32, (8, t), 0)
    cands, flats, valid = [], [], []
    for k, l0, cnt in _CAND_BLOCKS:
        cnd = a[k:k + 1, :] + b[l0:l0 + 8, :]
        if cnt < 8:
            cnd = jnp.where(sub < cnt, cnd, -jnp.inf)
        cands.append(cnd)
        flats.append(sub + (k * PEER_TOPK + l0))
        valid.append(None if cnt == 8 else sub < cnt)
    cands.append(a[8:16, :] + b[0:1, :])
    flats.append((sub + 8) * PEER_TOPK)
    valid.append(None)
    return cands, flats, valid


def _counts_from_sels(sels):
    t = sels[0].shape[1]
    row16 = lax.broadcasted_iota(jnp.int32, (PEER_TOPK, t), 0)
    selfs = [s_.astype(F32) for s_ in sels]
    per_k = [jnp.sum(selfs[0] + selfs[1], axis=0, keepdims=True)]
    per_k += [jnp.sum(s_, axis=0, keepdims=True) for s_ in selfs[2:9]]
    counts = jnp.concatenate([jnp.zeros((8, t), F32), selfs[9]], axis=0)
    for k in range(8):
        counts = jnp.where(row16 == k, per_k[k], counts)
    return counts


def _pair_counts(a, b):
    t = a.shape[1]
    cands, flats, _ = _candidates(a, b)
    top = a[0:1, :] + b[0:1, :]
    sels = [jnp.zeros((8, t), jnp.bool_) for _ in cands]
    zsum = jnp.zeros((1, t), F32)
    big = PEER_TOPK * PEER_TOPK
    for _ in range(PEER_TOPK):
        m = functools.reduce(jnp.maximum, cands)
        m = jnp.max(m, axis=0, keepdims=True)
        idx = functools.reduce(jnp.minimum,
                               [jnp.where(c == m, f, big) for c, f in zip(cands, flats)])
        idx = jnp.min(idx, axis=0, keepdims=True)
        hit = [f == idx for f in flats]
        cands = [jnp.where(h_, -jnp.inf, c) for h_, c in zip(hit, cands)]
        sels = [jnp.logical_or(s_, h_) for s_, h_ in zip(sels, hit)]
        zsum = zsum + jnp.exp(m - top)
    return _counts_from_sels(sels), zsum


def _pair_counts_distinct(a, b):
    cands, _, valid = _candidates(a, b)
    top = a[0:1, :] + b[0:1, :]
    zsum = jnp.zeros((1, a.shape[1]), F32)
    for _ in range(PEER_TOPK):
        m = functools.reduce(jnp.maximum, cands)
        m = jnp.max(m, axis=0, keepdims=True)
        cands = [jnp.where(c == m, -jnp.inf, c) for c in cands]
        zsum = zsum + jnp.exp(m - top)
    sels = [c == -jnp.inf if v is None else jnp.logical_and(c == -jnp.inf, v)
            for c, v in zip(cands, valid)]
    counts = _counts_from_sels(sels)
    total = jnp.sum(counts, axis=0, keepdims=True)
    return counts, zsum, (total != float(PEER_TOPK)).astype(jnp.int32)


def _route_head(s1, s2, exact):
    if exact:
        rank1, a = _topk_rank(s1)
        rank2, b = _topk_rank(s2)
        counts, zsum = _pair_counts(a, b)
        bad = None
    else:
        rank1, a, bad1 = _topk_rank_distinct(s1)
        rank2, b, bad2 = _topk_rank_distinct(s2)
        counts, zsum, bad3 = _pair_counts_distinct(a, b)
        bad = bad1 + bad2 + bad3
    idx = rank1.astype(jnp.int32)
    sub = idx & (SUBLANES - 1)
    lo = jnp.take_along_axis(counts[:SUBLANES], sub, axis=0)
    hi = jnp.take_along_axis(counts[SUBLANES:], sub, axis=0)
    nn = jnp.where(idx < SUBLANES, lo, jnp.where(idx < PEER_TOPK, hi, 0.0))
    r2 = rank2.astype(F32).astype(BF16)
    e2 = jnp.exp(s2 - b[0:1, :]).astype(BF16)
    p = jnp.exp(s1 - a[0:1, :]) / zsum
    return (r2, e2, nn, p), bad


def _route_kernel(x_ref, g_ref, wq_ref, sk_ref, hb_ref, r2_ref, e2_ref, nn_ref, p_ref, sc_ref):
    hb = _rms(x_ref[...], g_ref[...]).astype(BF16)
    hb_ref[...] = hb
    qt = lax.dot_general(wq_ref[...], hb, (((1,), (1,)), ((), ())),
                         preferred_element_type=F32).astype(BF16)
    for hp in range(2 * PEER_HEADS):
        sc_ref[hp] = jnp.dot(sk_ref[hp], qt[hp * PEER_HALF:(hp + 1) * PEER_HALF],
                             preferred_element_type=F32)
    slabs = nn_ref.shape[1]

    def head(h, carry):
        s1 = sc_ref[2 * h]
        s2 = sc_ref[2 * h + 1]

        def store(vals):
            r2, e2, nn, p = vals
            r2_ref[h] = r2
            e2_ref[h] = e2
            for c in range(slabs):
                nn_ref[h, c] = nn[:, c * LANES:(c + 1) * LANES]
                p_ref[h, c] = p[:, c * LANES:(c + 1) * LANES]

        vals, bad = _route_head(s1, s2, exact=False)
        store(vals)

        @pl.when(jnp.max(bad) > 0)
        def _():
            store(_route_head(s1, s2, exact=True)[0])

        return carry

    lax.fori_loop(0, PEER_HEADS, head, 0)


def _route_call(x, g, wq_t, sk, *, tile=512):
    n, d = x.shape
    hk = (PEER_HEADS, PEER_KEYS, n)
    hs = (PEER_HEADS, n // LANES, PEER_KEYS, LANES)
    blk = pl.BlockSpec((PEER_HEADS, PEER_KEYS, tile), lambda i: (0, 0, i))
    slab = pl.BlockSpec((PEER_HEADS, tile // LANES, PEER_KEYS, LANES), lambda i: (0, i, 0, 0))
    return pl.pallas_call(
        _route_kernel,
        grid=(n // tile,),
        in_specs=[pl.BlockSpec((tile, d), lambda i: (i, 0)),
                  pl.BlockSpec((1, d), lambda i: (0, 0)),
                  pl.BlockSpec(wq_t.shape, lambda i: (0, 0)),
                  pl.BlockSpec(sk.shape, lambda i: (0, 0, 0))],
        out_specs=[pl.BlockSpec((tile, d), lambda i: (i, 0)), blk, blk, slab, slab],
        out_shape=[jax.ShapeDtypeStruct((n, d), BF16),
                   jax.ShapeDtypeStruct(hk, BF16), jax.ShapeDtypeStruct(hk, BF16),
                   jax.ShapeDtypeStruct(hs, F32), jax.ShapeDtypeStruct(hs, F32)],
        scratch_shapes=[pltpu.VMEM((2 * PEER_HEADS, PEER_KEYS, tile), F32)],
        compiler_params=_params("parallel"),
        name="peer_route",
    )(x, g, wq_t, sk)


def _gelu_sig(x):
    k0 = -2.0 * math.sqrt(2.0 / math.pi) * math.log2(math.e)
    return x / (1.0 + jnp.exp2(x * (k0 + (k0 * 0.044715) * (x * x))))


def _row_bf16(ref, h, i, rows):
    parts = [jnp.broadcast_to(ref[h, c, i:i + 1, :], (BF16_ROWS, LANES)).astype(BF16)
             for c in range(ref.shape[1])]
    row = jnp.concatenate(parts, axis=1)
    return jnp.broadcast_to(row[None], (rows // BF16_ROWS, BF16_ROWS, row.shape[1])).reshape(
        rows, row.shape[1])


def _peer_kernel(x_ref, hb_ref, u_ref, vt_ref, r2_ref, e2_ref, nn_ref, p_ref, gf_ref, o_ref,
                 act_ref, w_ref, acc_ref, *, final_norm, act_splits):
    et = pl.program_id(1)
    te = u_ref.shape[0]
    tm = hb_ref.shape[0]
    rows_i = te // PEER_KEYS

    @pl.when(et == 0)
    def _():
        acc_ref[...] = jnp.zeros_like(acc_ref)

    def gating(ii):
        rs = slice(ii * PEER_KEYS, (ii + 1) * PEER_KEYS)
        gate = jnp.zeros((PEER_KEYS, tm), BF16)
        for h in range(PEER_HEADS):
            nn = _row_bf16(nn_ref, h, ii, PEER_KEYS)
            p = _row_bf16(p_ref, h, ii, PEER_KEYS)
            gate = gate + e2_ref[h] * jnp.where(r2_ref[h] < nn, p, jnp.zeros_like(p))
        w_ref[rs, :] = gate * _gelu_sig(act_ref[rs, :].astype(BF16))

    mrows = te // act_splits
    for m in range(act_splits):
        rs = slice(m * mrows, (m + 1) * mrows)
        act_ref[rs, :] = lax.dot_general(u_ref[rs, :], hb_ref[...], (((1,), (1,)), ((), ())),
                                         preferred_element_type=F32)
        for ii in range(m * rows_i // act_splits, (m + 1) * rows_i // act_splits):
            gating(ii)
    acc_ref[...] += jnp.dot(vt_ref[...], w_ref[...], preferred_element_type=F32)

    @pl.when(et == pl.num_programs(1) - 1)
    def _():
        y = x_ref[...] + acc_ref[...].T
        if final_norm:
            y = _rms(y, gf_ref[...])
        o_ref[...] = y


def _peer_call(x, hb, u, vt, r2, e2, nn, p, g_final, *, final_norm, tm=512, te=2048,
               act_splits=4):
    n, d = x.shape
    rows_i = te // PEER_KEYS
    tok = pl.BlockSpec((tm, d), lambda t, e: (t, 0))
    allj = pl.BlockSpec((PEER_HEADS, PEER_KEYS, tm), lambda t, e: (0, 0, t))
    rowi = pl.BlockSpec((PEER_HEADS, tm // LANES, rows_i, LANES), lambda t, e: (0, t, e, 0))
    return pl.pallas_call(
        functools.partial(_peer_kernel, final_norm=final_norm, act_splits=act_splits),
        grid=(n // tm, u.shape[0] // te),
        in_specs=[tok, tok,
                  pl.BlockSpec((te, d), lambda t, e: (e, 0)),
                  pl.BlockSpec((d, te), lambda t, e: (0, e)),
                  allj, allj, rowi, rowi,
                  pl.BlockSpec((1, d), lambda t, e: (0, 0))],
        out_specs=tok,
        out_shape=jax.ShapeDtypeStruct((n, d), F32),
        scratch_shapes=[pltpu.VMEM((te, tm), F32), pltpu.VMEM((te, tm), BF16),
                        pltpu.VMEM((d, tm), F32)],
        compiler_params=_params("parallel", "arbitrary"),
        name="peer_dense",
    )(x, hb, u, vt, r2, e2, nn, p, g_final)


def _peer_layer(x, g, w_q, sub_keys, u_tab, v_tab, g_final, final_norm):
    wq_t = w_q.T.astype(BF16)
    sk = sub_keys.reshape(PEER_HEADS * 2, PEER_KEYS, PEER_HALF).astype(BF16)
    hb, r2, e2, nn, p = _route_call(x, g, wq_t, sk)
    return _peer_call(x, hb, u_tab.astype(BF16), v_tab.T.astype(BF16), r2, e2, nn, p, g_final,
                      final_norm=final_norm)


def _rope_tables(seq):
    half = HEAD_DIM // 2
    inv = ROPE_THETA ** (-jnp.arange(half, dtype=F32) / half)
    t = jnp.arange(seq, dtype=F32).reshape(seq // BAND_MACRO, ATTN_BLOCK, 16)
    t = t.transpose(0, 2, 1).reshape(-1, ATTN_BLOCK)
    ang = t[:, :, None] * inv
    cos = jnp.tile(jnp.cos(ang), (1, 1, 4))
    sin = jnp.sin(ang)
    return cos, jnp.concatenate([-sin, -sin, sin, sin], axis=2)


def _attention_layer(x, g, w_qkv, w_o, *, batch, seq):
    n, d = x.shape
    perm = _rope_perm(d // HEAD_DIM)
    cos, sin = _rope_tables(seq)
    wg = w_qkv.reshape(d, len(DILATED_GROUPS), 3, d)
    outs, lses = [], []
    for gi, (window, dilation) in enumerate(DILATED_GROUPS):
        assert window // dilation == ATTN_BLOCK and 16 % dilation == 0
        w = jnp.concatenate([wg[:, gi, 0][:, perm], wg[:, gi, 1][:, perm], wg[:, gi, 2]],
                            axis=1).astype(BF16)
        dtype = BF16 if (ATTN_BLOCK * dilation // 16) % 16 == 0 else F32
        q, k, v = _qkv_call(x, g, w, cos, sin, batch=batch, dtype=dtype)
        o, lse = _attn_call(q, k, v, dilation=dilation, batch=batch, seq=seq)
        outs.append(o)
        lses.append(lse)
    return _attn_out_call(x, outs, lses, w_o.astype(BF16))


def kernel(x, norm_mix, norm_ffn, norm_final, s5_lam_re, s5_lam_im, s5_log_step, s5_b_re, s5_b_im, s5_c_re, s5_c_im, s5_d, s5_w_glu, attn_w_qkv, attn_w_o, peer_w_q, peer_sub_keys, peer_u, peer_v):
    batch, seq, d = x.shape
    assert batch == 8, "one timestep of all batches must fill one 8-sublane group"
    assert seq % BAND_MACRO == 0
    depth = norm_mix.shape[0]
    xs = x.transpose(1, 0, 2).reshape(seq * batch, d)
    band = False
    g_final = norm_final.reshape(1, d)
    for i in range(depth):
        j = i // 2
        g_mix = norm_mix[i].reshape(1, d)
        if i % 2 == 0:
            if band:
                xs = _from_band_order(xs, batch, seq).transpose(1, 0, 2).reshape(seq * batch, d)
                band = False
            wbu, are, aim, wc = _s5_weights(s5_lam_re[j], s5_lam_im[j], s5_log_step[j],
                                            s5_b_re[j], s5_b_im[j], s5_c_re[j], s5_c_im[j])
            xs = _s5_call(xs, g_mix, wbu, are, aim, wc, s5_d[j].reshape(1, d),
                          s5_w_glu[j].astype(BF16), batch=batch)
        else:
            if not band:
                xs = _to_band_order(xs, batch, seq)
                band = True
            xs = _attention_layer(xs, g_mix, attn_w_qkv[j], attn_w_o[j], batch=batch, seq=seq)
        xs = _peer_layer(xs, norm_ffn[i].reshape(1, d), peer_w_q[i], peer_sub_keys[i],
                         peer_u[i], peer_v[i], g_final, final_norm=(i == depth - 1))
    if band:
        return _from_band_order(xs, batch, seq)
    return xs.reshape(seq, batch, d).transpose(1, 0, 2)
```

```python
import functools
import math

import jax
import jax.numpy as jnp
import numpy as np
from jax import lax
from jax.experimental import pallas as pl
from jax.experimental.pallas import tpu as pltpu

F32 = jnp.float32
BF16 = jnp.bfloat16

RMS_EPS = 1e-6
SSM_GROUP = 16
SSM_STATE = 64
SSM_BLOCK_GROUPS = 16
HEAD_DIM = 64
DILATED_GROUPS = ((128, 1), (512, 4), (2048, 16))
ATTN_BLOCK = 128
ROPE_THETA = 10000.0
PEER_HEADS = 8
PEER_KEYS = 128
PEER_HALF = 128
PEER_TOPK = 16
NEG_BIG = -1e30

LANES = 128
SUBLANES = 8
BF16_ROWS = 16
MXU_WIDTH = 256
VMEM_LIMIT_BYTES = 56 * 1024 * 1024


def _params(*sem):
    return pltpu.CompilerParams(dimension_semantics=sem, vmem_limit_bytes=VMEM_LIMIT_BYTES)


def _rms(x, g):
    return x * lax.rsqrt(jnp.mean(x * x, axis=-1, keepdims=True) + RMS_EPS) * g


def _gelu(x):
    c = math.sqrt(2.0 / math.pi)
    return 0.5 * x * (1.0 + jnp.tanh(c * (x + 0.044715 * (x * x * x))))


def _s5_kernel(x_ref, g_ref, wbu_ref, are_ref, aim_ref, wc_ref, d_ref, y_ref, bu_ref, st_ref,
               *, batch, nblk):
    @pl.when(pl.program_id(0) == 0)
    def _():
        st_ref[...] = jnp.zeros_like(st_ref)

    rows = x_ref.shape[0]
    steps = rows // batch
    h = _rms(x_ref[...], g_ref[...])
    hb = h.astype(BF16)
    kin = wbu_ref.shape[1]
    half = wbu_ref.shape[2] // 2
    for c in range(nblk):
        bu_ref[...] = jnp.dot(hb[:, c * kin:(c + 1) * kin], wbu_ref[c],
                              preferred_element_type=F32)
        are = jnp.broadcast_to(are_ref[c], (batch, half))
        aim = jnp.broadcast_to(aim_ref[c], (batch, half))

        def step(t, carry):
            sre, sim = carry
            r = pl.multiple_of(t * batch, batch)
            bre = bu_ref[pl.ds(r, batch), 0:half]
            bim = bu_ref[pl.ds(r, batch), half:2 * half]
            nre = are * sre - aim * sim + bre
            nim = are * sim + aim * sre + bim
            bu_ref[pl.ds(r, batch), 0:half] = nre
            bu_ref[pl.ds(r, batch), half:2 * half] = nim
            return nre, nim

        sre, sim = lax.fori_loop(0, steps, step,
                                 (st_ref[c, :, 0:half], st_ref[c, :, half:2 * half]),
                                 unroll=True)
        st_ref[c, :, 0:half] = sre
        st_ref[c, :, half:2 * half] = sim
        yc = jnp.dot(bu_ref[...].astype(BF16), wc_ref[c], preferred_element_type=F32)
        yc = yc + d_ref[:, c * kin:(c + 1) * kin] * h[:, c * kin:(c + 1) * kin]
        y_ref[:, c * kin:(c + 1) * kin] = _gelu(yc).astype(BF16)


def _s5_weights(lam_re, lam_im, log_step, b_re, b_im, c_re, c_im):
    G, P = lam_re.shape
    H = b_re.shape[-1]
    step = jnp.exp(log_step)[:, None]
    mag = jnp.exp(lam_re * step)
    lb_re = mag * jnp.cos(lam_im * step)
    lb_im = mag * jnp.sin(lam_im * step)
    den = lam_re * lam_re + lam_im * lam_im
    num_re = lb_re - 1.0
    coef_re = (num_re * lam_re + lb_im * lam_im) / den
    coef_im = (lb_im * lam_re - num_re * lam_im) / den
    bb_re = coef_re[..., None] * b_re - coef_im[..., None] * b_im
    bb_im = coef_re[..., None] * b_im + coef_im[..., None] * b_re
    gb = SSM_BLOCK_GROUPS
    nblk = G // gb
    eye = jnp.eye(gb, dtype=F32)

    def bdiag_in(w):
        w = w.reshape(nblk, gb, P, H)
        return jnp.einsum('cgph,gk->cghkp', w, eye).reshape(nblk, gb * H, gb * P)

    def bdiag_out(w):
        w = w.reshape(nblk, gb, H, P)
        return jnp.einsum('cghp,gk->cgpkh', w, eye).reshape(nblk, gb * P, gb * H)

    wbu = jnp.concatenate([bdiag_in(bb_re), bdiag_in(bb_im)], axis=-1).astype(BF16)
    wc = jnp.concatenate([bdiag_out(c_re), -bdiag_out(c_im)], axis=1).astype(BF16)
    are = lb_re.reshape(nblk, 1, gb * P)
    aim = lb_im.reshape(nblk, 1, gb * P)
    return wbu, are, aim, wc


def _s5_call(x, g, wbu, are, aim, wc, d_skip, *, batch, steps_per_tile=64):
    n, d = x.shape
    nblk = wbu.shape[0]
    rows = batch * steps_per_tile
    assert n % rows == 0
    const3 = lambda i: (0, 0, 0)
    const2 = lambda i: (0, 0)
    return pl.pallas_call(
        functools.partial(_s5_kernel, batch=batch, nblk=nblk),
        grid=(n // rows,),
        in_specs=[pl.BlockSpec((rows, d), lambda i: (i, 0)),
                  pl.BlockSpec((1, d), const2),
                  pl.BlockSpec(wbu.shape, const3),
                  pl.BlockSpec(are.shape, const3),
                  pl.BlockSpec(aim.shape, const3),
                  pl.BlockSpec(wc.shape, const3),
                  pl.BlockSpec((1, d), const2)],
        out_specs=pl.BlockSpec((rows, d), lambda i: (i, 0)),
        out_shape=jax.ShapeDtypeStruct((n, d), BF16),
        scratch_shapes=[pltpu.VMEM((rows, wbu.shape[2]), F32),
                        pltpu.VMEM((nblk, batch, wbu.shape[2]), F32)],
        compiler_params=_params("arbitrary"),
        name="s5_ssm",
    )(x, g, wbu, are, aim, wc, d_skip)


def _glu_kernel(y_ref, x_ref, w_ref, o_ref):
    z = jnp.dot(y_ref[...], w_ref[...], preferred_element_type=F32)
    d = o_ref.shape[1]
    o_ref[...] = x_ref[...] + z[:, :d] * jax.nn.sigmoid(z[:, d:])


def _glu_call(y, x, w, *, tile=1024):
    n, d = x.shape
    return pl.pallas_call(
        _glu_kernel,
        grid=(n // tile,),
        in_specs=[pl.BlockSpec((tile, d), lambda i: (i, 0)),
                  pl.BlockSpec((tile, d), lambda i: (i, 0)),
                  pl.BlockSpec(w.shape, lambda i: (0, 0))],
        out_specs=pl.BlockSpec((tile, d), lambda i: (i, 0)),
        out_shape=jax.ShapeDtypeStruct((n, d), F32),
        compiler_params=_params("parallel"),
        name="s5_glu",
    )(y, x, w)


def _rope_perm(n_heads):
    half = HEAD_DIM // 2
    idx = []
    for p in range(n_heads // 2):
        h0, h1 = 2 * p, 2 * p + 1
        for part in (0, 1):
            for h in (h0, h1):
                idx.extend(range(h * HEAD_DIM + part * half, h * HEAD_DIM + (part + 1) * half))
    return np.asarray(idx, dtype=np.int32)


BAND_MACRO = ATTN_BLOCK * 16


def _to_band_order(x, batch, seq):
    d = x.shape[1]
    x = x.reshape(seq // BAND_MACRO, ATTN_BLOCK, 16, batch, d)
    return x.transpose(0, 2, 3, 1, 4).reshape(seq * batch, d)


def _from_band_order(x, batch, seq):
    d = x.shape[1]
    x = x.reshape(seq // BAND_MACRO, 16, batch, ATTN_BLOCK, d)
    return x.transpose(2, 0, 3, 1, 4).reshape(batch, seq, d)


def _qkv_kernel(x_ref, g_ref, w_ref, cos_ref, sin_ref, q_ref, k_ref, v_ref):
    d = x_ref.shape[1]
    hb = _rms(x_ref[...], g_ref[...]).astype(BF16)
    reps = x_ref.shape[0] // cos_ref.shape[0]
    cos = jnp.tile(cos_ref[...], (reps, 1))
    sin = jnp.tile(sin_ref[...], (reps, 1))
    scale = HEAD_DIM ** -0.5

    def roped(col0, ref, mul):
        for c in range(d // MXU_WIDTH):
            both = jnp.dot(hb, w_ref[:, col0 + c * MXU_WIDTH:col0 + (c + 1) * MXU_WIDTH],
                           preferred_element_type=F32)
            for half in range(MXU_WIDTH // LANES):
                blk = both[:, half * LANES:(half + 1) * LANES]
                out = blk * cos + pltpu.roll(blk, HEAD_DIM, axis=1) * sin
                if mul != 1.0:
                    out = out * mul
                lo = c * MXU_WIDTH + half * LANES
                ref[:, lo:lo + LANES] = out.astype(ref.dtype)

    roped(0, q_ref, scale)
    roped(d, k_ref, 1.0)
    v_ref[...] = jnp.dot(hb, w_ref[:, 2 * d:3 * d],
                         preferred_element_type=F32).astype(v_ref.dtype)


def _qkv_call(x, g, w, cos, sin, *, batch, dtype, tile=512):
    n, d = x.shape
    per_table = batch * ATTN_BLOCK
    assert per_table % tile == 0 and tile % ATTN_BLOCK == 0
    row = pl.BlockSpec((tile, d), lambda i: (i, 0))
    tab = pl.BlockSpec((None, ATTN_BLOCK, LANES), lambda i: (i * tile // per_table, 0, 0))
    out = jax.ShapeDtypeStruct((n, d), dtype)
    return pl.pallas_call(
        _qkv_kernel,
        grid=(n // tile,),
        in_specs=[row, pl.BlockSpec((1, d), lambda i: (0, 0)),
                  pl.BlockSpec(w.shape, lambda i: (0, 0)), tab, tab],
        out_specs=[row, row, row],
        out_shape=[out, out, out],
        compiler_params=_params("parallel"),
        name="attn_qkv",
    )(x, g, w, cos, sin)


ATTN_CLASSES_PER_STEP = 8


def _attn_kernel(q_ref, kp_ref, kc_ref, vp_ref, vc_ref, o_ref, l_ref, *, chunk):
    c = ATTN_BLOCK
    d = q_ref.shape[-1]
    first = pl.program_id(1) == 0

    def pos(r):
        return (r % chunk) * (c // chunk) + r // chunk

    qrow = lax.broadcasted_iota(jnp.int32, (c, 2 * c), 0)
    kcol = lax.broadcasted_iota(jnp.int32, (c, 2 * c), 1)
    cur = kcol >= c
    kpos = jnp.where(cur, pos(kcol - c) + c, pos(kcol))
    dist = pos(qrow) + c - kpos
    valid = (dist >= 0) & (dist <= c) & jnp.logical_or(cur, jnp.logical_not(first))
    lane = lax.broadcasted_iota(jnp.int32, (c, LANES), 1)
    qmask0 = (lane // (HEAD_DIM // 2)) % 2 == 0
    omask0 = lane < HEAD_DIM
    blk = (q_ref.shape[0], chunk, LANES)

    def load(ref, cl, sl):
        return ref[:, cl, :, sl].reshape(c, LANES).astype(BF16)

    for cl in range(q_ref.shape[1]):
        lse = jnp.zeros((c, LANES), F32)
        for p in range(d // LANES):
            sl = slice(p * LANES, (p + 1) * LANES)
            qp = load(q_ref, cl, sl)
            kp = jnp.concatenate([load(kp_ref, cl, sl), load(kc_ref, cl, sl)], axis=0)
            vp = jnp.concatenate([load(vp_ref, cl, sl), load(vc_ref, cl, sl)], axis=0)
            outs = []
            for e in range(2):
                qm = jnp.where(qmask0 if e == 0 else jnp.logical_not(qmask0), qp,
                               jnp.zeros_like(qp))
                s = lax.dot_general(qm, kp, (((1,), (1,)), ((), ())),
                                    preferred_element_type=F32)
                s = jnp.where(valid, s, NEG_BIG)
                smax = jnp.max(s, axis=-1, keepdims=True)
                ex = jnp.exp(s - smax)
                den = jnp.sum(ex, axis=-1, keepdims=True)
                o = jnp.dot(ex.astype(BF16), vp, preferred_element_type=F32)
                outs.append(o / den)
                lse = jnp.where(lane == 2 * p + e, smax + jnp.log(den), lse)
            o_ref[:, cl, :, sl] = jnp.where(omask0, outs[0], outs[1]).astype(
                o_ref.dtype).reshape(blk)
        l_ref[:, cl] = lse.reshape(blk)


def _attn_call(q, k, v, *, dilation, batch, seq):
    n, d = q.shape
    c = ATTN_BLOCK
    macros = seq // BAND_MACRO
    per = 16 // dilation
    chunk = c // per
    ncls = ATTN_CLASSES_PER_STEP
    view = (macros, per, dilation * batch, per, chunk, d)
    block = (None, per, ncls, None, chunk, d)

    def at(s, i):
        return (i // per, 0, s, i % per, 0, 0)

    cur = pl.BlockSpec(block, at)
    prev = pl.BlockSpec(block, lambda s, i: at(s, jnp.maximum(i - 1, 0)))
    lse_view = view[:-1] + (LANES,)
    o, lse = pl.pallas_call(
        functools.partial(_attn_kernel, chunk=chunk),
        grid=(dilation * batch // ncls, macros * per),
        in_specs=[cur, prev, cur, prev, cur],
        out_specs=[cur, pl.BlockSpec(block[:-1] + (LANES,), at)],
        out_shape=[jax.ShapeDtypeStruct(view, q.dtype), jax.ShapeDtypeStruct(lse_view, F32)],
        compiler_params=_params("parallel", "parallel"),
        name="attn_band",
    )(q.reshape(view), k.reshape(view), k.reshape(view), v.reshape(view), v.reshape(view))
    return o.reshape(n, d), lse.reshape(n, LANES)


def _attn_out_kernel(x_ref, o0, o1, o2, l0, l1, l2, e_ref, w_ref, y_ref):
    a, b, c = l0[...], l1[...], l2[...]
    mx = jnp.maximum(jnp.maximum(a, b), c)
    ea, eb, ec = jnp.exp(a - mx), jnp.exp(b - mx), jnp.exp(c - mx)
    inv = 1.0 / (ea + eb + ec)

    def spread(wt):
        hi = wt.astype(BF16)
        lo = (wt - hi.astype(F32)).astype(BF16)
        return (jnp.dot(hi, e_ref[...], preferred_element_type=F32)
                + jnp.dot(lo, e_ref[...], preferred_element_type=F32))

    o = spread(ea * inv) * o0[...].astype(F32) + spread(eb * inv) * o1[...].astype(F32) \
        + spread(ec * inv) * o2[...].astype(F32)
    y_ref[...] = x_ref[...] + jnp.dot(o.astype(BF16), w_ref[...], preferred_element_type=F32)


def _attn_out_call(x, os_, ls_, w, *, tile=1024):
    n, d = x.shape
    row = pl.BlockSpec((tile, d), lambda i: (i, 0))
    lrow = pl.BlockSpec((tile, LANES), lambda i: (i, 0))
    heads = jnp.arange(LANES, dtype=jnp.int32)[:, None]
    spread = (jnp.arange(d, dtype=jnp.int32)[None, :] // HEAD_DIM == heads).astype(BF16)
    return pl.pallas_call(
        _attn_out_kernel,
        grid=(n // tile,),
        in_specs=[row] * 4 + [lrow] * 3 + [pl.BlockSpec(spread.shape, lambda i: (0, 0)),
                                           pl.BlockSpec(w.shape, lambda i: (0, 0))],
        out_specs=row,
        out_shape=jax.ShapeDtypeStruct((n, d), F32),
        compiler_params=_params("parallel"),
        name="attn_out",
    )(x, *os_, *ls_, spread, w)


def _topk_rank(s):
    nk, t = s.shape
    iota = lax.broadcasted_iota(jnp.int32, (nk, t), 0)
    row16 = lax.broadcasted_iota(jnp.int32, (PEER_TOPK, t), 0)
    rank = jnp.full((nk, t), 99, jnp.int32)
    vals = jnp.zeros((PEER_TOPK, t), F32)
    for k in range(PEER_TOPK):
        m = jnp.max(s, axis=0, keepdims=True)
        idx = jnp.min(jnp.where(s == m, iota, nk), axis=0, keepdims=True)
        sel = iota == idx
        rank = jnp.where(sel, k, rank)
        s = jnp.where(sel, -jnp.inf, s)
        vals = jnp.where(row16 == k, m, vals)
    return rank, vals


SENTINEL_BASE = 1e38
SENTINEL_STEP = 1e37
SENTINEL_LIMIT = -0.95e38


def _topk_rank_distinct(s):
    nk, t = s.shape
    row16 = lax.broadcasted_iota(jnp.int32, (PEER_TOPK, t), 0)
    low = jnp.min(s, axis=0, keepdims=True)
    vals = jnp.zeros((PEER_TOPK, t), F32)
    for k in range(PEER_TOPK):
        m = jnp.max(s, axis=0, keepdims=True)
        s = jnp.where(s == m, -(SENTINEL_BASE + k * SENTINEL_STEP), s)
        vals = jnp.where(row16 == k, m, vals)
    top = s < SENTINEL_LIMIT
    rank = jnp.where(top, jnp.floor(s * (-1.0 / SENTINEL_STEP) - (SENTINEL_BASE / SENTINEL_STEP - 0.5)),
                     99.0)
    cnt = jnp.sum(top.astype(F32), axis=0, keepdims=True)
    bad = jnp.logical_or(cnt != float(PEER_TOPK), jnp.logical_not(low > SENTINEL_LIMIT))
    return rank, vals, bad.astype(jnp.int32)


_CAND_BLOCKS = ((0, 0, 8), (0, 8, 8), (1, 0, 8), (2, 0, 5), (3, 0, 4), (4, 0, 3), (5, 0, 2),
                (6, 0, 2), (7, 0, 2))


def _candidates(a, b):
    t = a.shape[1]
    sub = lax.broadcasted_iota(jnp.int32, (8, t), 0)
    cands, flats, valid = [], [], []
    for k, l0, cnt in _CAND_BLOCKS:
        cnd = a[k:k + 1, :] + b[l0:l0 + 8, :]
        if cnt < 8:
            cnd = jnp.where(sub < cnt, cnd, -jnp.inf)
        cands.append(cnd)
        flats.append(sub + (k * PEER_TOPK + l0))
        valid.append(None if cnt == 8 else sub < cnt)
    cands.append(a[8:16, :] + b[0:1, :])
    flats.append((sub + 8) * PEER_TOPK)
    valid.append(None)
    return cands, flats, valid


def _counts_from_sels(sels):
    t = sels[0].shape[1]
    row16 = lax.broadcasted_iota(jnp.int32, (PEER_TOPK, t), 0)
    selfs = [s_.astype(F32) for s_ in sels]
    per_k = [jnp.sum(selfs[0] + selfs[1], axis=0, keepdims=True)]
    per_k += [jnp.sum(s_, axis=0, keepdims=True) for s_ in selfs[2:9]]
    counts = jnp.concatenate([jnp.zeros((8, t), F32), selfs[9]], axis=0)
    for k in range(8):
        counts = jnp.where(row16 == k, per_k[k], counts)
    return counts


def _pair_counts(a, b):
    t = a.shape[1]
    cands, flats, _ = _candidates(a, b)
    top = a[0:1, :] + b[0:1, :]
    sels = [jnp.zeros((8, t), jnp.bool_) for _ in cands]
    zsum = jnp.zeros((1, t), F32)
    big = PEER_TOPK * PEER_TOPK
    for _ in range(PEER_TOPK):
        m = functools.reduce(jnp.maximum, cands)
        m = jnp.max(m, axis=0, keepdims=True)
        idx = functools.reduce(jnp.minimum,
                               [jnp.where(c == m, f, big) for c, f in zip(cands, flats)])
        idx = jnp.min(idx, axis=0, keepdims=True)
        hit = [f == idx for f in flats]
        cands = [jnp.where(h_, -jnp.inf, c) for h_, c in zip(hit, cands)]
        sels = [jnp.logical_or(s_, h_) for s_, h_ in zip(sels, hit)]
        zsum = zsum + jnp.exp(m - top)
    return _counts_from_sels(sels), zsum


def _pair_counts_distinct(a, b):
    cands, _, valid = _candidates(a, b)
    top = a[0:1, :] + b[0:1, :]
    zsum = jnp.zeros((1, a.shape[1]), F32)
    for _ in range(PEER_TOPK):
        m = functools.reduce(jnp.maximum, cands)
        m = jnp.max(m, axis=0, keepdims=True)
        cands = [jnp.where(c == m, -jnp.inf, c) for c in cands]
        zsum = zsum + jnp.exp(m - top)
    sels = [c == -jnp.inf if v is None else jnp.logical_and(c == -jnp.inf, v)
            for c, v in zip(cands, valid)]
    counts = _counts_from_sels(sels)
    total = jnp.sum(counts, axis=0, keepdims=True)
    return counts, zsum, (total != float(PEER_TOPK)).astype(jnp.int32)


def _route_head(s1, s2, exact):
    if exact:
        rank1, a = _topk_rank(s1)
        rank2, b = _topk_rank(s2)
        counts, zsum = _pair_counts(a, b)
        bad = None
    else:
        rank1, a, bad1 = _topk_rank_distinct(s1)
        rank2, b, bad2 = _topk_rank_distinct(s2)
        counts, zsum, bad3 = _pair_counts_distinct(a, b)
        bad = bad1 + bad2 + bad3
    idx = rank1.astype(jnp.int32)
    sub = idx & (SUBLANES - 1)
    lo = jnp.take_along_axis(counts[:SUBLANES], sub, axis=0)
    hi = jnp.take_along_axis(counts[SUBLANES:], sub, axis=0)
    nn = jnp.where(idx < SUBLANES, lo, jnp.where(idx < PEER_TOPK, hi, 0.0))
    r2 = rank2.astype(F32).astype(BF16)
    e2 = jnp.exp(s2 - b[0:1, :]).astype(BF16)
    p = jnp.exp(s1 - a[0:1, :]) / zsum
    return (r2, e2, nn, p), bad


def _route_kernel(x_ref, g_ref, wq_ref, sk_ref, hb_ref, r2_ref, e2_ref, nn_ref, p_ref, sc_ref):
    hb = _rms(x_ref[...], g_ref[...]).astype(BF16)
    hb_ref[...] = hb
    qt = lax.dot_general(wq_ref[...], hb, (((1,), (1,)), ((), ())),
                         preferred_element_type=F32).astype(BF16)
    for hp in range(2 * PEER_HEADS):
        sc_ref[hp] = jnp.dot(sk_ref[hp], qt[hp * PEER_HALF:(hp + 1) * PEER_HALF],
                             preferred_element_type=F32)
    slabs = nn_ref.shape[1]

    def head(h, carry):
        s1 = sc_ref[2 * h]
        s2 = sc_ref[2 * h + 1]

        def store(vals):
            r2, e2, nn, p = vals
            r2_ref[h] = r2
            e2_ref[h] = e2
            for c in range(slabs):
                nn_ref[h, c] = nn[:, c * LANES:(c + 1) * LANES]
                p_ref[h, c] = p[:, c * LANES:(c + 1) * LANES]

        vals, bad = _route_head(s1, s2, exact=False)
        store(vals)

        @pl.when(jnp.max(bad) > 0)
        def _():
            store(_route_head(s1, s2, exact=True)[0])

        return carry

    lax.fori_loop(0, PEER_HEADS, head, 0)


def _route_call(x, g, wq_t, sk, *, tile=512):
    n, d = x.shape
    hk = (PEER_HEADS, PEER_KEYS, n)
    hs = (PEER_HEADS, n // LANES, PEER_KEYS, LANES)
    blk = pl.BlockSpec((PEER_HEADS, PEER_KEYS, tile), lambda i: (0, 0, i))
    slab = pl.BlockSpec((PEER_HEADS, tile // LANES, PEER_KEYS, LANES), lambda i: (0, i, 0, 0))
    return pl.pallas_call(
        _route_kernel,
        grid=(n // tile,),
        in_specs=[pl.BlockSpec((tile, d), lambda i: (i, 0)),
                  pl.BlockSpec((1, d), lambda i: (0, 0)),
                  pl.BlockSpec(wq_t.shape, lambda i: (0, 0)),
                  pl.BlockSpec(sk.shape, lambda i: (0, 0, 0))],
        out_specs=[pl.BlockSpec((tile, d), lambda i: (i, 0)), blk, blk, slab, slab],
        out_shape=[jax.ShapeDtypeStruct((n, d), BF16),
                   jax.ShapeDtypeStruct(hk, BF16), jax.ShapeDtypeStruct(hk, BF16),
                   jax.ShapeDtypeStruct(hs, F32), jax.ShapeDtypeStruct(hs, F32)],
        scratch_shapes=[pltpu.VMEM((2 * PEER_HEADS, PEER_KEYS, tile), F32)],
        compiler_params=_params("parallel"),
        name="peer_route",
    )(x, g, wq_t, sk)


def _gelu_sig(x):
    k0 = -2.0 * math.sqrt(2.0 / math.pi) * math.log2(math.e)
    return x / (1.0 + jnp.exp2(x * (k0 + (k0 * 0.044715) * (x * x))))


def _row_bf16(ref, h, i, rows):
    parts = [jnp.broadcast_to(ref[h, c, i:i + 1, :], (BF16_ROWS, LANES)).astype(BF16)
             for c in range(ref.shape[1])]
    row = jnp.concatenate(parts, axis=1)
    return jnp.broadcast_to(row[None], (rows // BF16_ROWS, BF16_ROWS, row.shape[1])).reshape(
        rows, row.shape[1])


def _peer_kernel(x_ref, hb_ref, u_ref, vt_ref, r2_ref, e2_ref, nn_ref, p_ref, gf_ref, o_ref,
                 act_ref, w_ref, acc_ref, *, final_norm, act_splits):
    et = pl.program_id(1)
    te = u_ref.shape[0]
    tm = hb_ref.shape[0]
    rows_i = te // PEER_KEYS

    @pl.when(et == 0)
    def _():
        acc_ref[...] = jnp.zeros_like(acc_ref)

    def gating(ii):
        rs = slice(ii * PEER_KEYS, (ii + 1) * PEER_KEYS)
        gate = jnp.zeros((PEER_KEYS, tm), BF16)
        for h in range(PEER_HEADS):
            nn = _row_bf16(nn_ref, h, ii, PEER_KEYS)
            p = _row_bf16(p_ref, h, ii, PEER_KEYS)
            gate = gate + e2_ref[h] * jnp.where(r2_ref[h] < nn, p, jnp.zeros_like(p))
        w_ref[rs, :] = gate * _gelu_sig(act_ref[rs, :].astype(BF16))

    mrows = te // act_splits
    for m in range(act_splits):
        rs = slice(m * mrows, (m + 1) * mrows)
        act_ref[rs, :] = lax.dot_general(u_ref[rs, :], hb_ref[...], (((1,), (1,)), ((), ())),
                                         preferred_element_type=F32)
        for ii in range(m * rows_i // act_splits, (m + 1) * rows_i // act_splits):
            gating(ii)
    acc_ref[...] += jnp.dot(vt_ref[...], w_ref[...], preferred_element_type=F32)

    @pl.when(et == pl.num_programs(1) - 1)
    def _():
        y = x_ref[...] + acc_ref[...].T
        if final_norm:
            y = _rms(y, gf_ref[...])
        o_ref[...] = y


def _peer_call(x, hb, u, vt, r2, e2, nn, p, g_final, *, final_norm, tm=512, te=2048,
               act_splits=4):
    n, d = x.shape
    rows_i = te // PEER_KEYS
    tok = pl.BlockSpec((tm, d), lambda t, e: (t, 0))
    allj = pl.BlockSpec((PEER_HEADS, PEER_KEYS, tm), lambda t, e: (0, 0, t))
    rowi = pl.BlockSpec((PEER_HEADS, tm // LANES, rows_i, LANES), lambda t, e: (0, t, e, 0))
    return pl.pallas_call(
        functools.partial(_peer_kernel, final_norm=final_norm, act_splits=act_splits),
        grid=(n // tm, u.shape[0] // te),
        in_specs=[tok, tok,
                  pl.BlockSpec((te, d), lambda t, e: (e, 0)),
                  pl.BlockSpec((d, te), lambda t, e: (0, e)),
                  allj, allj, rowi, rowi,
                  pl.BlockSpec((1, d), lambda t, e: (0, 0))],
        out_specs=tok,
        out_shape=jax.ShapeDtypeStruct((n, d), F32),
        scratch_shapes=[pltpu.VMEM((te, tm), F32), pltpu.VMEM((te, tm), BF16),
                        pltpu.VMEM((d, tm), F32)],
        compiler_params=_params("parallel", "arbitrary"),
        name="peer_dense",
    )(x, hb, u, vt, r2, e2, nn, p, g_final)


def _peer_layer(x, g, w_q, sub_keys, u_tab, v_tab, g_final, final_norm):
    wq_t = w_q.T.astype(BF16)
    sk = sub_keys.reshape(PEER_HEADS * 2, PEER_KEYS, PEER_HALF).astype(BF16)
    hb, r2, e2, nn, p = _route_call(x, g, wq_t, sk)
    return _peer_call(x, hb, u_tab.astype(BF16), v_tab.T.astype(BF16), r2, e2, nn, p, g_final,
                      final_norm=final_norm)


def _rope_tables(seq):
    half = HEAD_DIM // 2
    inv = ROPE_THETA ** (-jnp.arange(half, dtype=F32) / half)
    t = jnp.arange(seq, dtype=F32).reshape(seq // BAND_MACRO, ATTN_BLOCK, 16)
    t = t.transpose(0, 2, 1).reshape(-1, ATTN_BLOCK)
    ang = t[:, :, None] * inv
    cos = jnp.tile(jnp.cos(ang), (1, 1, 4))
    sin = jnp.sin(ang)
    return cos, jnp.concatenate([-sin, -sin, sin, sin], axis=2)


def _attention_layer(x, g, w_qkv, w_o, *, batch, seq):
    n, d = x.shape
    perm = _rope_perm(d // HEAD_DIM)
    cos, sin = _rope_tables(seq)
    wg = w_qkv.reshape(d, len(DILATED_GROUPS), 3, d)
    outs, lses = [], []
    for gi, (window, dilation) in enumerate(DILATED_GROUPS):
        assert window // dilation == ATTN_BLOCK and 16 % dilation == 0
        w = jnp.concatenate([wg[:, gi, 0][:, perm], wg[:, gi, 1][:, perm], wg[:, gi, 2]],
                            axis=1).astype(BF16)
        dtype = BF16 if (ATTN_BLOCK * dilation // 16) % 16 == 0 else F32
        q, k, v = _qkv_call(x, g, w, cos, sin, batch=batch, dtype=dtype)
        o, lse = _attn_call(q, k, v, dilation=dilation, batch=batch, seq=seq)
        outs.append(o)
        lses.append(lse)
    return _attn_out_call(x, outs, lses, w_o.astype(BF16))


def kernel(x, norm_mix, norm_ffn, norm_final, s5_lam_re, s5_lam_im, s5_log_step, s5_b_re, s5_b_im, s5_c_re, s5_c_im, s5_d, s5_w_glu, attn_w_qkv, attn_w_o, peer_w_q, peer_sub_keys, peer_u, peer_v):
    batch, seq, d = x.shape
    assert batch == 8, "one timestep of all batches must fill one 8-sublane group"
    assert seq % BAND_MACRO == 0
    depth = norm_mix.shape[0]
    xs = x.transpose(1, 0, 2).reshape(seq * batch, d)
    band = False
    g_final = norm_final.reshape(1, d)
    for i in range(depth):
        j = i // 2
        g_mix = norm_mix[i].reshape(1, d)
        if i % 2 == 0:
            if band:
                xs = _from_band_order(xs, batch, seq).transpose(1, 0, 2).reshape(seq * batch, d)
                band = False
            wbu, are, aim, wc = _s5_weights(s5_lam_re[j], s5_lam_im[j], s5_log_step[j],
                                            s5_b_re[j], s5_b_im[j], s5_c_re[j], s5_c_im[j])
            y = _s5_call(xs, g_mix, wbu, are, aim, wc, s5_d[j].reshape(1, d), batch=batch)
            xs = _glu_call(y, xs, s5_w_glu[j].astype(BF16))
        else:
            if not band:
                xs = _to_band_order(xs, batch, seq)
                band = True
            xs = _attention_layer(xs, g_mix, attn_w_qkv[j], attn_w_o[j], batch=batch, seq=seq)
        xs = _peer_layer(xs, norm_ffn[i].reshape(1, d), peer_w_q[i], peer_sub_keys[i],
                         peer_u[i], peer_v[i], g_final, final_norm=(i == depth - 1))
    if band:
        return _from_band_order(xs, batch, seq)
    return xs.reshape(seq, batch, d).transpose(1, 0, 2)
```
